```python
import jax, jax.numpy as jnp
from jax import lax
import numpy as np

D_MODEL = 1024
BATCH = 8
SEQ = 4096
DEPTH = 2
DEC_BATCH = 16
DEC_SEQ = 2048
PAST_LEN = 128

GRID_W = 64
POOL_WIDTH = 512
POOL_GROUPS = 4
POOL_GROUP_DIM = POOL_WIDTH // POOL_GROUPS
POOL_WINDOWS = (2, 4, 8, 16)
MLSTM_HEADS = 4
MLSTM_HEAD_DIM = 128
MLSTM_WIDTH = MLSTM_HEADS * MLSTM_HEAD_DIM
MLSTM_CHUNK = 128
CONV_WIDTH = 3
N_DIRS = 2
IN0_SIZES = (POOL_WIDTH, POOL_WIDTH, MLSTM_WIDTH, MLSTM_WIDTH, MLSTM_WIDTH, MLSTM_WIDTH, N_DIRS * MLSTM_HEADS, N_DIRS * MLSTM_HEADS)
IN0_WIDTH = sum(IN0_SIZES)
IN0_SPLIT_POINTS = tuple(int(s) for s in np.cumsum(IN0_SIZES)[:-1])
OUT0_WIDTH = POOL_WIDTH + MLSTM_WIDTH
ATTN_HEADS = 8
KV_HEADS = 2
HEAD_DIM = 128
ATTN_WIDTH = ATTN_HEADS * HEAD_DIM
KV_WIDTH = KV_HEADS * HEAD_DIM
Q_BLOCK = 128
ROPE_THETA = 10000.0
AXIS_DIM = HEAD_DIM // 2
IN1_SIZES = (ATTN_WIDTH, KV_WIDTH, KV_WIDTH, ATTN_WIDTH)
IN1_WIDTH = sum(IN1_SIZES)
IN1_SPLIT_POINTS = tuple(int(s) for s in np.cumsum(IN1_SIZES)[:-1])
ALPHA = (2 * DEPTH) ** 0.25
BETA = (8 * DEPTH) ** -0.25
LN_EPS = 1e-5
RMS_EPS = 1e-6
M_INIT = -1e30
N_EVEN = (DEPTH + 1) // 2
N_ODD = DEPTH // 2

kernel_name = 'hybrid_pool_mlstm_axial_gqa_encoder'


def layer_norm(x, g, b):
    xf = x.astype(jnp.float32)
    mu = xf.mean(-1, keepdims=True)
    var = jnp.square(xf - mu).mean(-1, keepdims=True)
    return ((xf - mu) * lax.rsqrt(var + LN_EPS) * g + b).astype(x.dtype)


def rms_norm(x, g):
    xf = x.astype(jnp.float32)
    return (xf * lax.rsqrt(jnp.square(xf).mean(-1, keepdims=True) + RMS_EPS) * g).astype(x.dtype)


def centred_pool_mixer(xa, w_pool, pool_scale):
    B, T, _ = xa.shape
    xf = xa.astype(jnp.float32)
    cs = jnp.concatenate([jnp.zeros((B, 1, POOL_WIDTH), jnp.float32), jnp.cumsum(xf, axis=1)], axis=1)
    t = jnp.arange(T)
    groups = []
    for g, w in enumerate(POOL_WINDOWS):
        left = w // 2
        right = w - 1 - left
        lo = jnp.clip(t - left, 0, T)
        hi = jnp.clip(t + right + 1, 0, T)
        sl = slice(g * POOL_GROUP_DIM, (g + 1) * POOL_GROUP_DIM)
        seg = cs[:, :, sl]
        cnt = (hi - lo).astype(jnp.float32)
        mean = (seg[:, hi] - seg[:, lo]) / cnt[None, :, None]
        groups.append(mean - xf[:, :, sl])
    pooled = jnp.stack(groups, axis=2).astype(xa.dtype)
    mixed = jnp.einsum('btgc,gcd->btgd', pooled, w_pool)
    return mixed.reshape(B, T, POOL_WIDTH) * pool_scale


def centred_depthwise_conv(x, w, b):
    T = x.shape[1]
    pad = CONV_WIDTH // 2
    xp = jnp.pad(x, ((0, 0), (pad, CONV_WIDTH - 1 - pad), (0, 0)))
    return sum(xp[:, j:j + T] * w[j] for j in range(CONV_WIDTH)) + b


def mlstm_chunkwise(q, k, v, ig, lf):
    B, T, H, dh = q.shape
    L = MLSTM_CHUNK
    nc = T // L

    def chunks(a):
        return jnp.moveaxis(a.reshape((B, nc, L) + a.shape[2:]), 1, 0).swapaxes(2, 3)

    qc = chunks(q.astype(jnp.float32))
    kc = chunks(k.astype(jnp.float32) * (dh ** -0.5))
    vc = chunks(v.astype(jnp.float32))
    ic = chunks(ig)
    fc = chunks(lf)
    causal = jnp.tril(jnp.ones((L, L), dtype=bool))

    def step(carry, inp):
        C, n, m = carry
        qj, kj, vj, ij, fj = inp
        g = jnp.cumsum(fj, axis=-1)
        G = g[..., -1]
        dmat = jnp.where(causal, g[..., :, None] - g[..., None, :] + ij[..., None, :], -jnp.inf)
        inter = g + m[..., None]
        m_row = jnp.maximum(inter, dmat.max(-1))
        s = jnp.einsum('bhld,bhsd->bhls', qj, kj) * jnp.exp(dmat - m_row[..., None])
        ex = jnp.exp(inter - m_row)
        num = jnp.einsum('bhls,bhsd->bhld', s, vj) + ex[..., None] * jnp.einsum('bhvk,bhlk->bhlv', C, qj)
        den = s.sum(-1) + ex * jnp.einsum('bhk,bhlk->bhl', n, qj)
        h = num / jnp.maximum(jnp.abs(den), jnp.exp(-m_row))[..., None]
        w_s = G[..., None] - g + ij
        m_new = jnp.maximum(G + m, w_s.max(-1))
        decay = jnp.exp(G + m - m_new)
        ws = jnp.exp(w_s - m_new[..., None])
        C_new = decay[..., None, None] * C + jnp.einsum('bhs,bhsv,bhsk->bhvk', ws, vj, kj)
        n_new = decay[..., None] * n + jnp.einsum('bhs,bhsk->bhk', ws, kj)
        return (C_new, n_new, m_new), h

    init = (jnp.zeros((B, H, dh, dh), jnp.float32), jnp.zeros((B, H, dh), jnp.float32), jnp.full((B, H), M_INIT, jnp.float32))
    _, h = lax.scan(step, init, (qc, kc, vc, ic, fc))
    return jnp.moveaxis(h.swapaxes(2, 3), 0, 1).reshape(B, T, H, dh)


def even_mixer(x, w_in, w_pool, pool_scale, conv_w, conv_b, w_q, w_k, b_gate_i, b_gate_f, mh_norm_g, skip, w_out):
    B, T, _ = x.shape
    H, dh = MLSTM_HEADS, MLSTM_HEAD_DIM
    proj = x @ w_in
    xa, za, xb, vb, ob, zb, gi, gf = jnp.split(proj, IN0_SPLIT_POINTS, axis=-1)
    out_a = centred_pool_mixer(xa, w_pool, pool_scale) * jax.nn.silu(za)
    xc = jax.nn.silu(centred_depthwise_conv(xb, conv_w, conv_b))
    xch = xc.reshape(B, T, H, dh)
    q = jnp.einsum('bthd,hde->bthe', xch, w_q)
    k = jnp.einsum('bthd,hde->bthe', xch, w_k)
    v = vb.reshape(B, T, H, dh)
    ig = (gi.reshape(B, T, N_DIRS, H) + b_gate_i).astype(jnp.float32)
    lf = jax.nn.log_sigmoid((gf.reshape(B, T, N_DIRS, H) + b_gate_f).astype(jnp.float32))
    rev = lambda a: jnp.flip(a, axis=1)
    h_fwd = mlstm_chunkwise(q, k, v, ig[:, :, 0], lf[:, :, 0])
    h_bwd = rev(mlstm_chunkwise(rev(q), rev(k), rev(v), rev(ig[:, :, 1]), rev(lf[:, :, 1])))
    h = (h_fwd + h_bwd) * jax.nn.sigmoid(ob.astype(jnp.float32)).reshape(B, T, H, dh)
    mu = h.mean(-1, keepdims=True)
    var = jnp.square(h - mu).mean(-1, keepdims=True)
    hn = ((h - mu) * lax.rsqrt(var + LN_EPS)).reshape(B, T, MLSTM_WIDTH) * mh_norm_g
    out_b = (hn.astype(x.dtype) + skip * xc) * jax.nn.silu(zb)
    return jnp.concatenate([out_a, out_b], axis=-1) @ w_out


def axial_rope_tables(T):
    rows = T // GRID_W
    t_row = jnp.repeat(jnp.arange(rows, dtype=jnp.float32), GRID_W)
    t_col = jnp.tile(jnp.arange(GRID_W, dtype=jnp.float32), rows)
    n_freq = AXIS_DIM // 2
    inv = ROPE_THETA ** (-jnp.arange(n_freq, dtype=jnp.float32) / n_freq)
    ang = jnp.stack([t_row[:, None] * inv, t_col[:, None] * inv], axis=1)
    return jnp.cos(ang), jnp.sin(ang)


def apply_axial_rope(x, cos, sin):
    B, T, H, _ = x.shape
    n_freq = AXIS_DIM // 2
    xr = x.reshape(B, T, H, 2, 2, n_freq)
    x1, x2 = xr[..., 0, :], xr[..., 1, :]
    c = cos.astype(x.dtype)[None, :, None]
    s = sin.astype(x.dtype)[None, :, None]
    out = jnp.stack([x1 * c - x2 * s, x2 * c + x1 * s], axis=-2)
    return out.reshape(B, T, H, HEAD_DIM)


def blocked_attention(q, k, v):
    B, T = q.shape[:2]
    nblk = T // Q_BLOCK
    grp = ATTN_HEADS // KV_HEADS
    qb = q.reshape(B, nblk, Q_BLOCK, KV_HEADS, grp, HEAD_DIM).transpose(1, 0, 2, 3, 4, 5)
    scale = HEAD_DIM ** -0.5

    def one_block(qblk):
        s = jnp.einsum('bqkgd,bskd->bkgqs', qblk, k).astype(jnp.float32) * scale
        p = jax.nn.softmax(s, axis=-1).astype(v.dtype)
        return jnp.einsum('bkgqs,bskd->bqkgd', p, v)

    o = lax.map(one_block, qb)
    return o.transpose(1, 0, 2, 3, 4, 5).reshape(B, T, ATTN_WIDTH)


def odd_mixer(x, w_in, q_norm_g, k_norm_g, w_out):
    B, T, _ = x.shape
    proj = x @ w_in
    q, k, v, z = jnp.split(proj, IN1_SPLIT_POINTS, axis=-1)
    q = rms_norm(q.reshape(B, T, ATTN_HEADS, HEAD_DIM), q_norm_g)
    k = rms_norm(k.reshape(B, T, KV_HEADS, HEAD_DIM), k_norm_g)
    v = v.reshape(B, T, KV_HEADS, HEAD_DIM)
    cos, sin = axial_rope_tables(T)
    q = apply_axial_rope(q, cos, sin)
    k = apply_axial_rope(k, cos, sin)
    o = blocked_attention(q, k, v)
    return (o * jax.nn.silu(z)) @ w_out


def trunk(x, w_in_even, w_pool, pool_scale, conv_w, conv_b, w_q_m, w_k_m, b_gate_i, b_gate_f, mh_norm_g, skip, w_out_even, w_in_odd, q_norm_g, k_norm_g, w_out_odd, ln_g, ln_b):
    for layer in range(DEPTH):
        j = layer // 2
        if layer % 2 == 0:
            mix = even_mixer(x, w_in_even[j], w_pool[j], pool_scale[j], conv_w[j], conv_b[j], w_q_m[j], w_k_m[j], b_gate_i[j], b_gate_f[j], mh_norm_g[j], skip[j], w_out_even[j])
        else:
            mix = odd_mixer(x, w_in_odd[j], q_norm_g[j], k_norm_g[j], w_out_odd[j])
        x = layer_norm(ALPHA * x + mix, ln_g[layer], ln_b[layer])
    return x


def setup_inputs(seed: int = 0) -> dict:
    key = jax.random.key(seed)
    ks = jax.random.split(key, 24)
    f32 = jnp.float32
    nrm = lambda k, shape: jax.random.normal(k, shape, f32)
    return {
        'x_prompt': nrm(ks[0], (BATCH, SEQ, D_MODEL)),
        'x_sample': nrm(ks[1], (DEC_BATCH, DEC_SEQ, D_MODEL)),
        'w_in_even': nrm(ks[2], (N_EVEN, D_MODEL, IN0_WIDTH)) * D_MODEL ** -0.5,
        'w_pool': nrm(ks[3], (N_EVEN, POOL_GROUPS, POOL_GROUP_DIM, POOL_GROUP_DIM)) * POOL_GROUP_DIM ** -0.5,
        'pool_scale': 1.0 + 0.1 * nrm(ks[4], (N_EVEN, POOL_WIDTH)),
        'conv_w': nrm(ks[5], (N_EVEN, CONV_WIDTH, MLSTM_WIDTH)) * CONV_WIDTH ** -0.5,
        'conv_b': 0.02 * nrm(ks[6], (N_EVEN, MLSTM_WIDTH)),
        'w_q_m': nrm(ks[7], (N_EVEN, MLSTM_HEADS, MLSTM_HEAD_DIM, MLSTM_HEAD_DIM)) * MLSTM_HEAD_DIM ** -0.5,
        'w_k_m': nrm(ks[8], (N_EVEN, MLSTM_HEADS, MLSTM_HEAD_DIM, MLSTM_HEAD_DIM)) * MLSTM_HEAD_DIM ** -0.5,
        'b_gate_i': 0.1 * nrm(ks[9], (N_EVEN, N_DIRS, MLSTM_HEADS)),
        'b_gate_f': 3.0 + 3.0 * jax.random.uniform(ks[10], (N_EVEN, N_DIRS, MLSTM_HEADS), f32),
        'mh_norm_g': 1.0 + 0.02 * nrm(ks[11], (N_EVEN, MLSTM_WIDTH)),
        'skip': 1.0 + 0.1 * nrm(ks[12], (N_EVEN, MLSTM_WIDTH)),
        'w_out_even': nrm(ks[13], (N_EVEN, OUT0_WIDTH, D_MODEL)) * (OUT0_WIDTH ** -0.5 * BETA),
        'w_in_odd': nrm(ks[14], (N_ODD, D_MODEL, IN1_WIDTH)) * D_MODEL ** -0.5,
        'q_norm_g': 1.0 + 0.02 * nrm(ks[15], (N_ODD, HEAD_DIM)),
        'k_norm_g': 1.0 + 0.02 * nrm(ks[16], (N_ODD, HEAD_DIM)),
        'w_out_odd': nrm(ks[17], (N_ODD, ATTN_WIDTH, D_MODEL)) * (ATTN_WIDTH ** -0.5 * BETA),
        'ln_g': 1.0 + 0.02 * nrm(ks[18], (DEPTH, D_MODEL)),
        'ln_b': 0.02 * nrm(ks[19], (DEPTH, D_MODEL)),
    }


def reference(x_prompt, x_sample, w_in_even, w_pool, pool_scale, conv_w, conv_b, w_q_m, w_k_m, b_gate_i, b_gate_f, mh_norm_g, skip, w_out_even, w_in_odd, q_norm_g, k_norm_g, w_out_odd, ln_g, ln_b):
    y_prompt = trunk(x_prompt, w_in_even, w_pool, pool_scale, conv_w, conv_b, w_q_m, w_k_m, b_gate_i, b_gate_f, mh_norm_g, skip, w_out_even, w_in_odd, q_norm_g, k_norm_g, w_out_odd, ln_g, ln_b)
    y_sample = trunk(x_sample, w_in_even, w_pool, pool_scale, conv_w, conv_b, w_q_m, w_k_m, b_gate_i, b_gate_f, mh_norm_g, skip, w_out_even, w_in_odd, q_norm_g, k_norm_g, w_out_odd, ln_g, ln_b)
    return (y_prompt, y_sample)
```

```python
import functools

import jax
import jax.numpy as jnp
import numpy as np
from jax import lax
from jax.experimental import pallas as pl
from jax.experimental.pallas import tpu as pltpu

D_MODEL = 1024
GRID_W = 64
POOL_WIDTH = 512
POOL_GROUPS = 4
POOL_GROUP_DIM = POOL_WIDTH // POOL_GROUPS
POOL_WINDOWS = (2, 4, 8, 16)
MLSTM_HEADS = 4
MLSTM_HEAD_DIM = 128
MLSTM_WIDTH = MLSTM_HEADS * MLSTM_HEAD_DIM
MLSTM_CHUNK = 128
N_DIRS = 2
ATTN_HEADS = 8
KV_HEADS = 2
HEAD_DIM = 128
ATTN_WIDTH = ATTN_HEADS * HEAD_DIM
KV_WIDTH = KV_HEADS * HEAD_DIM
GQA_GROUP = ATTN_HEADS // KV_HEADS
ROPE_THETA = 10000.0
AXIS_DIM = HEAD_DIM // 2
DEPTH = 2
ALPHA = (2 * DEPTH) ** 0.25
LN_EPS = 1e-5
RMS_EPS = 1e-6
M_INIT = -1e30

LANES = 128
HALO = 8
GATE_ROWS = 16
VMEM_LIMIT = 56 * 1024 * 1024

F32 = jnp.float32
BF16 = jnp.bfloat16


def _dot(a, b):
    return jnp.dot(a, b, preferred_element_type=F32)


def _dot_nt(a, b):
    return lax.dot_general(a, b, (((1,), (1,)), ((), ())), preferred_element_type=F32)


def _sigmoid(x):
    return 1.0 / (1.0 + jnp.exp(-x))


def _silu(x):
    return x * _sigmoid(x)


def _log_sigmoid(x):
    return jnp.minimum(x, 0.0) - jnp.log1p(jnp.exp(-jnp.abs(x)))


def _params(semantics):
    return pltpu.CompilerParams(dimension_semantics=semantics, vmem_limit_bytes=VMEM_LIMIT)


def _inproj0_kernel(x_ref, w_ref, wg_ref, bg_ref, tri_ref, ones_ref,
                    xa_ref, zas_ref, xb_ref, v_ref, so_ref, zs_ref, gcol_ref, grow_ref):
    x = x_ref[...].astype(BF16)
    W = POOL_WIDTH
    xa_ref[...] = _dot(x, w_ref[:, 0 * W:1 * W])
    zas_ref[...] = _silu(_dot(x, w_ref[:, 1 * W:2 * W])).astype(BF16)
    xb_ref[...] = _dot(x, w_ref[:, 2 * W:3 * W])
    v_ref[...] = _dot(x, w_ref[:, 3 * W:4 * W]).astype(BF16)
    so_ref[...] = _sigmoid(_dot(x, w_ref[:, 4 * W:5 * W])).astype(BF16)
    zs_ref[...] = _silu(_dot(x, w_ref[:, 5 * W:6 * W])).astype(BF16)

    pre = _dot(x, wg_ref[...])
    ig = pre + bg_ref[0:1, :]
    lf = _log_sigmoid(pltpu.roll(pre, LANES - 8, axis=1) + bg_ref[1:2, :])
    hi = lf.astype(BF16)
    lo = (lf - hi.astype(F32)).astype(BF16)
    g = _dot(tri_ref[...], hi) + _dot(tri_ref[...], lo)
    tot = _dot(ones_ref[...], hi) + _dot(ones_ref[...], lo)
    lane = lax.broadcasted_iota(jnp.int32, g.shape, 1)
    gsel = jnp.where(lane < MLSTM_HEADS, g, tot - g + lf)
    r = ig - gsel
    for h in range(MLSTM_HEADS):
        sh = (LANES - h) % LANES
        gcol_ref[:, h * LANES:(h + 1) * LANES] = pltpu.roll(gsel, sh, axis=1) if sh else gsel
        rr = pltpu.roll(r, sh, axis=1) if sh else r
        tt = pltpu.roll(tot, (sh + 8) % LANES, axis=1)
        cm = jnp.where(lane < 8, rr, tt)
        grow_ref[h * GATE_ROWS:(h + 1) * GATE_ROWS, :] = cm.T[0:GATE_ROWS, :]


def _inproj0(x2d, w_main, w_gate, b_gate, tri, ones_bd, tm):
    n = x2d.shape[0]
    W = POOL_WIDTH
    tok = lambda width: pl.BlockSpec((tm, width), lambda i: (i, 0))
    full = lambda a: pl.BlockSpec(a.shape, lambda i: (0,) * a.ndim)
    return pl.pallas_call(
        _inproj0_kernel,
        grid=(n // tm,),
        in_specs=[tok(D_MODEL), full(w_main), full(w_gate), full(b_gate), full(tri), full(ones_bd)],
        out_specs=[tok(W), tok(W), tok(W), tok(W), tok(W), tok(W), tok(MLSTM_HEADS * LANES),
                   pl.BlockSpec((MLSTM_HEADS * GATE_ROWS, tm), lambda i: (0, i))],
        out_shape=[
            jax.ShapeDtypeStruct((n, W), F32),
            jax.ShapeDtypeStruct((n, W), BF16),
            jax.ShapeDtypeStruct((n, W), F32),
            jax.ShapeDtypeStruct((n, W), BF16),
            jax.ShapeDtypeStruct((n, W), BF16),
            jax.ShapeDtypeStruct((n, W), BF16),
            jax.ShapeDtypeStruct((n, MLSTM_HEADS * LANES), F32),
            jax.ShapeDtypeStruct((MLSTM_HEADS * GATE_ROWS, n), F32),
        ],
        compiler_params=_params(("parallel",)),
        name="inproj_even",
    )(x2d, w_main, w_gate, b_gate, tri, ones_bd)


def _mix0_kernel(xa_ref, xap_ref, xan_ref, xb_ref, xbp_ref, xbn_ref, zas_ref,
                 wpool_ref, pscale_ref, convw_ref, convb_ref, wq_ref, wkt_ref,
                 outa_ref, q_ref, kt_ref, xc_ref, exta_ref, extb_ref, *, seq_len, tm):
    i = pl.program_id(1)
    first = i == 0
    last = i == pl.num_programs(1) - 1
    for ext, main, prev, nxt in ((exta_ref, xa_ref, xap_ref, xan_ref), (extb_ref, xb_ref, xbp_ref, xbn_ref)):
        ext[0:HALO, :] = jnp.where(first, 0.0, prev[...])
        ext[HALO:HALO + tm, :] = main[...]
        ext[HALO + tm:HALO + tm + HALO, :] = jnp.where(last, 0.0, nxt[...])

    t = i * tm + lax.broadcasted_iota(jnp.int32, (tm, 1), 0)
    GD = POOL_GROUP_DIM
    for g, w in enumerate(POOL_WINDOWS):
        left = w // 2
        right = w - 1 - left
        cols = slice(g * GD, (g + 1) * GD)
        acc = exta_ref[HALO - left:HALO - left + tm, cols]
        for off in range(-left + 1, right + 1):
            acc = acc + exta_ref[HALO + off:HALO + off + tm, cols]
        cnt = (jnp.minimum(t + right + 1, seq_len) - jnp.maximum(t - left, 0)).astype(F32)
        pooled = acc / cnt - xa_ref[:, cols]
        mixed = _dot(pooled.astype(BF16), wpool_ref[g])
        outa_ref[:, cols] = (mixed * pscale_ref[:, cols] * zas_ref[:, cols].astype(F32)).astype(BF16)

    conv = (extb_ref[HALO - 1:HALO - 1 + tm, :] * convw_ref[0:1, :]
            + extb_ref[HALO:HALO + tm, :] * convw_ref[1:2, :]
            + extb_ref[HALO + 1:HALO + 1 + tm, :] * convw_ref[2:3, :]
            + convb_ref[...])
    xc = _silu(conv)
    xc_ref[...] = xc.astype(BF16)
    xcb = xc.astype(BF16)
    dh = MLSTM_HEAD_DIM
    for h in range(MLSTM_HEADS):
        cols = slice(h * dh, (h + 1) * dh)
        q_ref[:, cols] = _dot(xcb[:, cols], wq_ref[h]).astype(BF16)
        kt_ref[cols, :] = (_dot_nt(wkt_ref[h], xcb[:, cols]) * (dh ** -0.5)).astype(BF16)


def _mix0(xa, xb, zas, w_pool, pool_scale, conv_w, conv_b, w_q, w_kt, batch, seq_len, tm):
    n = xa.shape[0]
    W = POOL_WIDTH
    nt = seq_len // tm
    hb = tm // HALO
    nhalo = n // HALO
    main = pl.BlockSpec((tm, W), lambda b, i: (b * nt + i, 0))
    prev = pl.BlockSpec((HALO, W), lambda b, i: (jnp.maximum((b * nt + i) * hb - 1, 0), 0))
    nxt = pl.BlockSpec((HALO, W), lambda b, i: (jnp.minimum((b * nt + i + 1) * hb, nhalo - 1), 0))
    full = lambda a: pl.BlockSpec(a.shape, lambda b, i: (0,) * a.ndim)
    return pl.pallas_call(
        functools.partial(_mix0_kernel, seq_len=seq_len, tm=tm),
        grid=(batch, nt),
        in_specs=[main, prev, nxt, main, prev, nxt, main,
                  full(w_pool), full(pool_scale), full(conv_w), full(conv_b), full(w_q), full(w_kt)],
        out_specs=[main, main, pl.BlockSpec((W, tm), lambda b, i: (b, i)), main],
        out_shape=[
            jax.ShapeDtypeStruct((n, W), BF16),
            jax.ShapeDtypeStruct((n, W), BF16),
            jax.ShapeDtypeStruct((batch * W, seq_len), BF16),
            jax.ShapeDtypeStruct((n, W), BF16),
        ],
        scratch_shapes=[pltpu.VMEM((tm + 2 * HALO, W), F32), pltpu.VMEM((tm + 2 * HALO, W), F32)],
        compiler_params=_params(("parallel", "parallel")),
        name="mix_even",
    )(xa, xa, xa, xb, xb, xb, zas, w_pool, pool_scale, conv_w, conv_b, w_q, w_kt)


def _mlstm_chunk(q, kt, v, gc, r_row, g_row, st_ref, m, mask):
    L = MLSTM_CHUNK
    one_col = (lax.broadcasted_iota(jnp.int32, (L, LANES), 1) == 0).astype(BF16)
    vext = jnp.concatenate([v, one_col], axis=1)
    st = st_ref[...]
    qk = _dot(q, kt)
    inter = _dot(q, st.astype(BF16))
    dm = jnp.where(mask, gc + r_row, -jnp.inf)
    inter_m = gc + m
    m_row = jnp.maximum(inter_m, jnp.max(dm, axis=-1, keepdims=True))
    p = qk * jnp.exp(dm - m_row)
    intra = _dot(p.astype(BF16), vext)
    ex = jnp.exp(inter_m - m_row)
    num = intra[:, :LANES] + ex * inter[:, :LANES]
    den = intra[:, LANES:LANES + 1] + ex * inter[:, LANES:LANES + 1]
    h = num / jnp.maximum(jnp.abs(den), jnp.exp(-m_row))
    big_g = g_row[:, 0:1]
    m_new = big_g + jnp.maximum(m, jnp.max(r_row, axis=-1, keepdims=True))
    decay = jnp.exp(big_g + m - m_new)
    ws = jnp.exp(big_g + r_row - m_new)
    st_ref[...] = decay * st + _dot((kt.astype(F32) * ws).astype(BF16), vext)
    return h, m_new


def _mlstm_kernel(q_ref, kt_ref, v_ref, gcol_ref, grow_ref, so_ref, xc_ref, zs_ref, ng_ref, skip_ref,
                  out_ref, stf_ref, stb_ref, hf_ref, hb_ref, *, seq_len):
    L = MLSTM_CHUNK
    nc = seq_len // L
    stf_ref[...] = jnp.zeros_like(stf_ref)
    stb_ref[...] = jnp.zeros_like(stb_ref)
    row = lax.broadcasted_iota(jnp.int32, (L, L), 0)
    col = lax.broadcasted_iota(jnp.int32, (L, L), 1)
    causal = col <= row
    anti = col >= row

    def body(j, carry):
        m_f, m_b = carry
        of = pl.multiple_of(j * L, L)
        ob = pl.multiple_of((nc - 1 - j) * L, L)
        h_f, m_f = _mlstm_chunk(
            q_ref[pl.ds(of, L), :], kt_ref[:, pl.ds(of, L)], v_ref[pl.ds(of, L), :],
            gcol_ref[pl.ds(of, L), 0:1], grow_ref[0:1, pl.ds(of, L)], grow_ref[8:9, pl.ds(of, L)],
            stf_ref, m_f, causal)
        hf_ref[pl.ds(of, L), :] = h_f
        h_b, m_b = _mlstm_chunk(
            q_ref[pl.ds(ob, L), :], kt_ref[:, pl.ds(ob, L)], v_ref[pl.ds(ob, L), :],
            gcol_ref[pl.ds(ob, L), 4:5], grow_ref[4:5, pl.ds(ob, L)], grow_ref[12:13, pl.ds(ob, L)],
            stb_ref, m_b, anti)
        hb_ref[pl.ds(ob, L), :] = h_b
        return m_f, m_b

    m0 = jnp.full((1, 1), M_INIT, F32)
    lax.fori_loop(0, nc, body, (m0, m0))

    rows = 512

    def fin(j, _):
        o = pl.multiple_of(j * rows, rows)
        sl = pl.ds(o, rows)
        h = (hf_ref[sl, :] + hb_ref[sl, :]) * so_ref[sl, :].astype(F32)
        mu = jnp.mean(h, axis=-1, keepdims=True)
        d = h - mu
        var = jnp.mean(d * d, axis=-1, keepdims=True)
        hn = d * lax.rsqrt(var + LN_EPS) * ng_ref[...]
        out = (hn + skip_ref[...] * xc_ref[sl, :].astype(F32)) * zs_ref[sl, :].astype(F32)
        out_ref[sl, :] = out.astype(BF16)
        return 0

    lax.fori_loop(0, seq_len // rows, fin, 0)


def _mlstm(q, kt, v, gcol, grow, so, xc, zs, norm_g, skip, batch, seq_len):
    n = q.shape[0]
    H = MLSTM_HEADS
    dh = MLSTM_HEAD_DIM
    tok = pl.BlockSpec((seq_len, dh), lambda b, h: (b, h))
    vec = pl.BlockSpec((1, dh), lambda b, h: (0, h))
    return pl.pallas_call(
        functools.partial(_mlstm_kernel, seq_len=seq_len),
        grid=(batch, H),
        in_specs=[tok, pl.BlockSpec((dh, seq_len), lambda b, h: (b * H + h, 0)), tok,
                  tok, pl.BlockSpec((GATE_ROWS, seq_len), lambda b, h: (h, b)),
                  tok, tok, tok, vec, vec],
        out_specs=tok,
        out_shape=jax.ShapeDtypeStruct((n, H * dh), BF16),
        scratch_shapes=[pltpu.VMEM((dh, 2 * LANES), F32), pltpu.VMEM((dh, 2 * LANES), F32),
                        pltpu.VMEM((seq_len, dh), F32), pltpu.VMEM((seq_len, dh), F32)],
        compiler_params=_params(("parallel", "parallel")),
        name="mlstm",
    )(q, kt, v, gcol, grow, so, xc, zs, norm_g, skip)


def _outproj_ln_kernel(a_ref, b_ref, wa_ref, wb_ref, x_ref, g_ref, beta_ref, y_ref):
    mix = _dot(a_ref[...], wa_ref[...]) + _dot(b_ref[...], wb_ref[...])
    z = ALPHA * x_ref[...] + mix
    mu = jnp.mean(z, axis=-1, keepdims=True)
    d = z - mu
    var = jnp.mean(d * d, axis=-1, keepdims=True)
    y_ref[...] = d * lax.rsqrt(var + LN_EPS) * g_ref[...] + beta_ref[...]


def _outproj_ln(a, a_blk, b, b_blk, w_a, w_b, x2d, ln_g, ln_b, tm):
    n = x2d.shape[0]
    half = w_a.shape[0]
    full = lambda arr: pl.BlockSpec(arr.shape, lambda i: (0,) * arr.ndim)
    return pl.pallas_call(
        _outproj_ln_kernel,
        grid=(n // tm,),
        in_specs=[pl.BlockSpec((tm, half), lambda i: (i, a_blk)),
                  pl.BlockSpec((tm, half), lambda i: (i, b_blk)),
                  full(w_a), full(w_b),
                  pl.BlockSpec((tm, D_MODEL), lambda i: (i, 0)), full(ln_g), full(ln_b)],
        out_specs=pl.BlockSpec((tm, D_MODEL), lambda i: (i, 0)),
        out_shape=jax.ShapeDtypeStruct((n, D_MODEL), F32),
        compiler_params=_params(("parallel",)),
        name="outproj_ln",
    )(a, b, w_a, w_b, x2d, ln_g, ln_b)


def _rope(xn, cos, sin_signed, first_half):
    partner = jnp.where(first_half, pltpu.roll(xn, LANES - AXIS_DIM // 2, axis=1),
                        pltpu.roll(xn, AXIS_DIM // 2, axis=1))
    return xn * cos + partner * sin_signed


def _inproj1_kernel(x_ref, w_ref, cos_ref, sin_ref, qg_ref, kg_ref, q_ref, k_ref, vext_ref, zs_ref):
    x = x_ref[...].astype(BF16)
    cos = cos_ref[...]
    sin = sin_ref[...]
    tm = x.shape[0]
    lane = lax.broadcasted_iota(jnp.int32, (tm, LANES), 1)
    first_half = (lane % AXIS_DIM) < (AXIS_DIM // 2)
    dh = HEAD_DIM

    def norm_rope(y, gain, scale):
        ms = jnp.mean(y * y, axis=-1, keepdims=True)
        yn = y * lax.rsqrt(ms + RMS_EPS) * gain
        out = _rope(yn, cos, sin, first_half)
        return out * scale if scale is not None else out

    for s in range(ATTN_WIDTH // 512):
        y = _dot(x, w_ref[:, s * 512:(s + 1) * 512])
        for h in range(512 // dh):
            cols = slice(s * 512 + h * dh, s * 512 + (h + 1) * dh)
            q_ref[:, cols] = norm_rope(y[:, h * dh:(h + 1) * dh], qg_ref[...], dh ** -0.5).astype(BF16)
    kv = _dot(x, w_ref[:, ATTN_WIDTH:ATTN_WIDTH + 2 * KV_WIDTH])
    one_col = (lane == 0).astype(BF16)
    for h in range(KV_HEADS):
        k_ref[:, h * dh:(h + 1) * dh] = norm_rope(kv[:, h * dh:(h + 1) * dh], kg_ref[...], None).astype(BF16)
        vext_ref[:, 2 * h * dh:(2 * h + 1) * dh] = kv[:, KV_WIDTH + h * dh:KV_WIDTH + (h + 1) * dh].astype(BF16)
        vext_ref[:, (2 * h + 1) * dh:(2 * h + 2) * dh] = one_col
    z0 = ATTN_WIDTH + 2 * KV_WIDTH
    for s in range(ATTN_WIDTH // 512):
        zs_ref[:, s * 512:(s + 1) * 512] = _silu(_dot(x, w_ref[:, z0 + s * 512:z0 + (s + 1) * 512])).astype(BF16)


def _inproj1(x2d, w, cos, sin, q_gain, k_gain, seq_len, tm):
    n = x2d.shape[0]
    nt = seq_len // tm
    tok = lambda width: pl.BlockSpec((tm, width), lambda i: (i, 0))
    pos = pl.BlockSpec((tm, LANES), lambda i: (i % nt, 0))
    full = lambda a: pl.BlockSpec(a.shape, lambda i: (0,) * a.ndim)
    return pl.pallas_call(
        _inproj1_kernel,
        grid=(n // tm,),
        in_specs=[tok(D_MODEL), full(w), pos, pos, full(q_gain), full(k_gain)],
        out_specs=[tok(ATTN_WIDTH), tok(KV_WIDTH), tok(2 * KV_WIDTH), tok(ATTN_WIDTH)],
        out_shape=[
            jax.ShapeDtypeStruct((n, ATTN_WIDTH), BF16),
            jax.ShapeDtypeStruct((n, KV_WIDTH), BF16),
            jax.ShapeDtypeStruct((n, 2 * KV_WIDTH), BF16),
            jax.ShapeDtypeStruct((n, ATTN_WIDTH), BF16),
        ],
        compiler_params=_params(("parallel",)),
        name="inproj_odd",
    )(x2d, w, cos, sin, q_gain, k_gain)


def _attn_kernel(q_ref, k_ref, vext_ref, zs_ref, o_ref, acc_ref, m_ref, *, seq_len, tq, tk):
    dh = HEAD_DIM
    nk = seq_len // tk
    for h in range(GQA_GROUP):
        q = q_ref[:, h * dh:(h + 1) * dh]
        acc_ref[...] = jnp.zeros_like(acc_ref)
        m_ref[...] = jnp.full_like(m_ref, -jnp.inf)

        def body(j, _):
            o = pl.multiple_of(j * tk, tk)
            s = _dot_nt(q, k_ref[pl.ds(o, tk), :])
            m_old = m_ref[...]
            m_new = jnp.maximum(m_old, jnp.max(s, axis=-1, keepdims=True))
            p = jnp.exp(s - m_new)
            alpha = jnp.exp(m_old - m_new)
            acc_ref[...] = alpha * acc_ref[...] + _dot(p.astype(BF16), vext_ref[pl.ds(o, tk), :])
            m_ref[...] = m_new
            return 0

        lax.fori_loop(0, nk, body, 0)
        acc = acc_ref[...]
        out = acc[:, :dh] / acc[:, dh:dh + 1]
        o_ref[:, h * dh:(h + 1) * dh] = (out * zs_ref[:, h * dh:(h + 1) * dh].astype(F32)).astype(BF16)


def _attention(q, k, vext, zs, batch, seq_len, tq, tk):
    n = q.shape[0]
    dh = HEAD_DIM
    nq = seq_len // tq
    gw = GQA_GROUP * dh
    qspec = pl.BlockSpec((tq, gw), lambda b, g, i: (b * nq + i, g))
    return pl.pallas_call(
        functools.partial(_attn_kernel, seq_len=seq_len, tq=tq, tk=tk),
        grid=(batch, KV_HEADS, nq),
        in_specs=[qspec,
                  pl.BlockSpec((seq_len, dh), lambda b, g, i: (b, g)),
                  pl.BlockSpec((seq_len, 2 * dh), lambda b, g, i: (b, g)),
                  qspec],
        out_specs=qspec,
        out_shape=jax.ShapeDtypeStruct((n, ATTN_WIDTH), BF16),
        scratch_shapes=[pltpu.VMEM((tq, 2 * dh), F32), pltpu.VMEM((tq, 1), F32)],
        compiler_params=_params(("parallel", "parallel", "parallel")),
        name="gqa_attn",
    )(q, k, vext, zs)


def _rope_tables(seq_len):
    t = np.arange(seq_len)
    n_freq = AXIS_DIM // 2
    inv = jnp.asarray(ROPE_THETA, F32) ** (-jnp.arange(n_freq, dtype=F32) / n_freq)
    t_row = jnp.asarray(t // GRID_W, F32)
    t_col = jnp.asarray(t % GRID_W, F32)
    ang = jnp.stack([t_row[:, None] * inv, t_col[:, None] * inv], axis=1)
    cos = jnp.cos(ang)
    sin = jnp.sin(ang)
    cos_full = jnp.concatenate([cos, cos], axis=-1).reshape(seq_len, HEAD_DIM)
    sin_signed = jnp.concatenate([-sin, sin], axis=-1).reshape(seq_len, HEAD_DIM)
    return cos_full, sin_signed


def _chunk_matrices(tm):
    idx = np.arange(tm)
    same = (idx[:, None] // MLSTM_CHUNK) == (idx[None, :] // MLSTM_CHUNK)
    tri = same & (idx[None, :] <= idx[:, None])
    return jnp.asarray(tri, BF16), jnp.asarray(same, BF16)


def _pad_lanes(a):
    return jnp.pad(a, ((0, 0), (0, LANES - a.shape[-1])))


def _even_layer(x2d, batch, seq_len, p, j, layer):
    W = POOL_WIDTH
    tm = 512
    w_in = p['w_in_even'][j]
    w_main = w_in[:, :6 * W].astype(BF16)
    w_gate = _pad_lanes(w_in[:, 6 * W:]).astype(BF16)
    b_gate = jnp.stack([_pad_lanes(p['b_gate_i'][j].reshape(1, -1))[0],
                        _pad_lanes(p['b_gate_f'][j].reshape(1, -1))[0]]).astype(F32)
    tri, ones_bd = _chunk_matrices(tm)
    xa, zas, xb, v, so, zs, gcol, grow = _inproj0(x2d, w_main, w_gate, b_gate, tri, ones_bd, tm)
    out_a, q, kt, xc = _mix0(
        xa, xb, zas, p['w_pool'][j].astype(BF16), p['pool_scale'][j].reshape(1, W),
        p['conv_w'][j], p['conv_b'][j].reshape(1, W), p['w_q_m'][j].astype(BF16),
        jnp.swapaxes(p['w_k_m'][j], 1, 2).astype(BF16), batch, seq_len, tm)
    out_b = _mlstm(q, kt, v, gcol, grow, so, xc, zs, p['mh_norm_g'][j].reshape(1, W),
                   p['skip'][j].reshape(1, W), batch, seq_len)
    w_out = p['w_out_even'][j].astype(BF16)
    return _outproj_ln(out_a, 0, out_b, 0, w_out[:W], w_out[W:], x2d,
                       p['ln_g'][layer].reshape(1, -1), p['ln_b'][layer].reshape(1, -1), 1024)


def _odd_layer(x2d, batch, seq_len, p, j, layer):
    cos, sin = _rope_tables(seq_len)
    q, k, vext, zs = _inproj1(x2d, p['w_in_odd'][j].astype(BF16), cos, sin,
                              p['q_norm_g'][j].reshape(1, -1), p['k_norm_g'][j].reshape(1, -1), seq_len, 512)
    o = _attention(q, k, vext, zs, batch, seq_len, 256, 512)
    w_out = p['w_out_odd'][j].astype(BF16)
    half = ATTN_WIDTH // 2
    return _outproj_ln(o, 0, o, 1, w_out[:half], w_out[half:], x2d,
                       p['ln_g'][layer].reshape(1, -1), p['ln_b'][layer].reshape(1, -1), 1024)


def _trunk(x, p):
    batch, seq_len, _ = x.shape
    x2d = x.reshape(batch * seq_len, D_MODEL)
    for layer in range(DEPTH):
        j = layer // 2
        if layer % 2 == 0:
            x2d = _even_layer(x2d, batch, seq_len, p, j, layer)
        else:
            x2d = _odd_layer(x2d, batch, seq_len, p, j, layer)
    return x2d.reshape(batch, seq_len, D_MODEL)


def kernel(x_prompt, x_sample, w_in_even, w_pool, pool_scale, conv_w, conv_b, w_q_m, w_k_m, b_gate_i, b_gate_f, mh_norm_g, skip, w_out_even, w_in_odd, q_norm_g, k_norm_g, w_out_odd, ln_g, ln_b):
    p = dict(w_in_even=w_in_even, w_pool=w_pool, pool_scale=pool_scale, conv_w=conv_w, conv_b=conv_b,
             w_q_m=w_q_m, w_k_m=w_k_m, b_gate_i=b_gate_i, b_gate_f=b_gate_f, mh_norm_g=mh_norm_g,
             skip=skip, w_out_even=w_out_even, w_in_odd=w_in_odd, q_norm_g=q_norm_g, k_norm_g=k_norm_g,
             w_out_odd=w_out_odd, ln_g=ln_g, ln_b=ln_b)
    return _trunk(x_prompt, p), _trunk(x_sample, p)
```

```python
import functools

import jax
import jax.numpy as jnp
import numpy as np
from jax import lax
from jax.experimental import pallas as pl
from jax.experimental.pallas import tpu as pltpu

D_MODEL = 1024
GRID_W = 64
POOL_WIDTH = 512
POOL_GROUPS = 4
POOL_GROUP_DIM = POOL_WIDTH // POOL_GROUPS
POOL_WINDOWS = (2, 4, 8, 16)
MLSTM_HEADS = 4
MLSTM_HEAD_DIM = 128
MLSTM_WIDTH = MLSTM_HEADS * MLSTM_HEAD_DIM
MLSTM_CHUNK = 128
N_DIRS = 2
ATTN_HEADS = 8
KV_HEADS = 2
HEAD_DIM = 128
ATTN_WIDTH = ATTN_HEADS * HEAD_DIM
KV_WIDTH = KV_HEADS * HEAD_DIM
GQA_GROUP = ATTN_HEADS // KV_HEADS
ROPE_THETA = 10000.0
AXIS_DIM = HEAD_DIM // 2
DEPTH = 2
ALPHA = (2 * DEPTH) ** 0.25
LN_EPS = 1e-5
RMS_EPS = 1e-6
M_INIT = -1e30
LOG2E = 1.4426950408889634

LANES = 128
HALO = 8
GATE_ROWS = 24
MLSTM_UNROLL = 4
VMEM_LIMIT = 56 * 1024 * 1024

F32 = jnp.float32
BF16 = jnp.bfloat16


def _dot(a, b):
    return jnp.dot(a, b, preferred_element_type=F32)


def _dot_nt(a, b):
    return lax.dot_general(a, b, (((1,), (1,)), ((), ())), preferred_element_type=F32)


def _sigmoid(x):
    return 1.0 / (1.0 + jnp.exp(-x))


def _silu(x):
    return x * _sigmoid(x)


def _log_sigmoid(x):
    return jnp.minimum(x, 0.0) - jnp.log1p(jnp.exp(-jnp.abs(x)))


def _params(semantics):
    return pltpu.CompilerParams(dimension_semantics=semantics, vmem_limit_bytes=VMEM_LIMIT)


def _inproj0_kernel(x_ref, w_ref, wvt_ref, wg_ref, bg_ref, tri_ref, ones_ref,
                    xa_ref, zas_ref, xb_ref, vt_ref, so_ref, zs_ref, gcol_ref, grow_ref):
    x = x_ref[...].astype(BF16)
    W = POOL_WIDTH
    xa_ref[...] = _dot(x, w_ref[:, 0 * W:1 * W])
    zas_ref[...] = _silu(_dot(x, w_ref[:, 1 * W:2 * W])).astype(BF16)
    xb_ref[...] = _dot(x, w_ref[:, 2 * W:3 * W])
    vt_ref[...] = _dot_nt(wvt_ref[...], x).astype(BF16)
    so_ref[...] = _sigmoid(_dot(x, w_ref[:, 3 * W:4 * W])).astype(BF16)
    zs_ref[...] = _silu(_dot(x, w_ref[:, 4 * W:5 * W])).astype(BF16)

    pre = _dot(x, wg_ref[...])
    ig = pre + bg_ref[0:1, :]
    lf = _log_sigmoid(pltpu.roll(pre, LANES - 8, axis=1) + bg_ref[1:2, :])
    hi = lf.astype(BF16)
    lo = (lf - hi.astype(F32)).astype(BF16)
    g = _dot(tri_ref[...], hi) + _dot(tri_ref[...], lo)
    tot = _dot(ones_ref[...], hi) + _dot(ones_ref[...], lo)
    lane = lax.broadcasted_iota(jnp.int32, g.shape, 1)
    gsel = jnp.where(lane < MLSTM_HEADS, g, tot - g + lf)
    r = ig - gsel
    for h in range(MLSTM_HEADS):
        sh = (LANES - h) % LANES
        rr = pltpu.roll(r, sh, axis=1) if sh else r
        gcol_ref[:, h * LANES:(h + 1) * LANES] = rr
        tt = pltpu.roll(tot, (sh + 8) % LANES, axis=1)
        gg = pltpu.roll(gsel, (sh + 16) % LANES, axis=1)
        cm = jnp.where(lane < 8, rr, jnp.where(lane < 16, tt, gg))
        grow_ref[h * GATE_ROWS:(h + 1) * GATE_ROWS, :] = cm.T[0:GATE_ROWS, :]


def _inproj0(x2d, w_main, w_vt, w_gate, b_gate, tri, ones_bd, tm):
    n = x2d.shape[0]
    W = POOL_WIDTH
    tok = lambda width: pl.BlockSpec((tm, width), lambda i: (i, 0))
    full = lambda a: pl.BlockSpec(a.shape, lambda i: (0,) * a.ndim)
    return pl.pallas_call(
        _inproj0_kernel,
        grid=(n // tm,),
        in_specs=[tok(D_MODEL), full(w_main), full(w_vt), full(w_gate), full(b_gate), full(tri), full(ones_bd)],
        out_specs=[tok(W), tok(W), tok(W), pl.BlockSpec((W, tm), lambda i: (0, i)), tok(W), tok(W),
                   tok(MLSTM_HEADS * LANES),
                   pl.BlockSpec((MLSTM_HEADS * GATE_ROWS, tm), lambda i: (0, i))],
        out_shape=[
            jax.ShapeDtypeStruct((n, W), F32),
            jax.ShapeDtypeStruct((n, W), BF16),
            jax.ShapeDtypeStruct((n, W), F32),
            jax.ShapeDtypeStruct((W, n), BF16),
            jax.ShapeDtypeStruct((n, W), BF16),
            jax.ShapeDtypeStruct((n, W), BF16),
            jax.ShapeDtypeStruct((n, MLSTM_HEADS * LANES), F32),
            jax.ShapeDtypeStruct((MLSTM_HEADS * GATE_ROWS, n), F32),
        ],
        compiler_params=_params(("parallel",)),
        name="inproj_even",
    )(x2d, w_main, w_vt, w_gate, b_gate, tri, ones_bd)


def _mix0_kernel(xa_ref, xap_ref, xan_ref, xb_ref, xbp_ref, xbn_ref, zas_ref,
                 wpool_ref, pscale_ref, convw_ref, convb_ref, wk_ref, wqt_ref,
                 outa_ref, k_ref, qt_ref, xc_ref, exta_ref, extb_ref, *, seq_len, tm):
    i = pl.program_id(1)
    first = i == 0
    last = i == pl.num_programs(1) - 1
    for ext, main, prev, nxt in ((exta_ref, xa_ref, xap_ref, xan_ref), (extb_ref, xb_ref, xbp_ref, xbn_ref)):
        ext[0:HALO, :] = jnp.where(first, 0.0, prev[...])
        ext[HALO:HALO + tm, :] = main[...]
        ext[HALO + tm:HALO + tm + HALO, :] = jnp.where(last, 0.0, nxt[...])

    t = i * tm + lax.broadcasted_iota(jnp.int32, (tm, 1), 0)
    GD = POOL_GROUP_DIM
    for g, w in enumerate(POOL_WINDOWS):
        left = w // 2
        right = w - 1 - left
        cols = slice(g * GD, (g + 1) * GD)
        acc = exta_ref[HALO - left:HALO - left + tm, cols]
        for off in range(-left + 1, right + 1):
            acc = acc + exta_ref[HALO + off:HALO + off + tm, cols]
        cnt = (jnp.minimum(t + right + 1, seq_len) - jnp.maximum(t - left, 0)).astype(F32)
        pooled = acc / cnt - xa_ref[:, cols]
        mixed = _dot(pooled.astype(BF16), wpool_ref[g])
        outa_ref[:, cols] = (mixed * pscale_ref[:, cols] * zas_ref[:, cols].astype(F32)).astype(BF16)

    conv = (extb_ref[HALO - 1:HALO - 1 + tm, :] * convw_ref[0:1, :]
            + extb_ref[HALO:HALO + tm, :] * convw_ref[1:2, :]
            + extb_ref[HALO + 1:HALO + 1 + tm, :] * convw_ref[2:3, :]
            + convb_ref[...])
    xc = _silu(conv)
    xc_ref[...] = xc.astype(BF16)
    xcb = xc.astype(BF16)
    dh = MLSTM_HEAD_DIM
    for h in range(MLSTM_HEADS):
        cols = slice(h * dh, (h + 1) * dh)
        k_ref[:, cols] = (_dot(xcb[:, cols], wk_ref[h]) * (dh ** -0.5)).astype(BF16)
        qt_ref[cols, :] = _dot_nt(wqt_ref[h], xcb[:, cols]).astype(BF16)


def _mix0(xa, xb, zas, w_pool, pool_scale, conv_w, conv_b, w_k, w_qt, batch, seq_len, tm):
    n = xa.shape[0]
    W = POOL_WIDTH
    nt = seq_len // tm
    hb = tm // HALO
    nhalo = n // HALO
    main = pl.BlockSpec((tm, W), lambda b, i: (b * nt + i, 0))
    prev = pl.BlockSpec((HALO, W), lambda b, i: (jnp.maximum((b * nt + i) * hb - 1, 0), 0))
    nxt = pl.BlockSpec((HALO, W), lambda b, i: (jnp.minimum((b * nt + i + 1) * hb, nhalo - 1), 0))
    full = lambda a: pl.BlockSpec(a.shape, lambda b, i: (0,) * a.ndim)
    return pl.pallas_call(
        functools.partial(_mix0_kernel, seq_len=seq_len, tm=tm),
        grid=(batch, nt),
        in_specs=[main, prev, nxt, main, prev, nxt, main,
                  full(w_pool), full(pool_scale), full(conv_w), full(conv_b), full(w_k), full(w_qt)],
        out_specs=[main, main, pl.BlockSpec((W, tm), lambda b, i: (b, i)), main],
        out_shape=[
            jax.ShapeDtypeStruct((n, W), BF16),
            jax.ShapeDtypeStruct((n, W), BF16),
            jax.ShapeDtypeStruct((batch * W, seq_len), BF16),
            jax.ShapeDtypeStruct((n, W), BF16),
        ],
        scratch_shapes=[pltpu.VMEM((tm + 2 * HALO, W), F32), pltpu.VMEM((tm + 2 * HALO, W), F32)],
        compiler_params=_params(("parallel", "parallel")),
        name="mix_even",
    )(xa, xa, xa, xb, xb, xb, zas, w_pool, pool_scale, conv_w, conv_b, w_k, w_qt)


def _mlstm_chunk(k, qt, vt, r_col, r_row, tot_row, g_row, st, m, mask):
    L = MLSTM_CHUNK
    one_row = (lax.broadcasted_iota(jnp.int32, (LANES, L), 0) == 0).astype(BF16)
    vext = jnp.concatenate([vt, one_row], axis=0)
    both = _dot(jnp.concatenate([k, st.astype(BF16)], axis=0), qt)
    kq = both[:L]
    inter = both[L:]
    dm = jnp.where(mask, r_col + g_row, -jnp.inf)
    inter_m = g_row + m
    m_row = jnp.maximum(inter_m, jnp.max(dm, axis=0, keepdims=True))
    p = kq * jnp.exp(dm - m_row)
    intra = _dot(vext, p.astype(BF16))
    ex = jnp.exp(inter_m - m_row)
    num = intra[:LANES] + ex * inter[:LANES]
    den = intra[LANES:LANES + 1] + ex * inter[LANES:LANES + 1]
    h_t = num / jnp.maximum(jnp.abs(den), jnp.exp(-m_row))
    big_g = tot_row[:, 0:1]
    m_new = big_g + jnp.maximum(m, jnp.max(r_row, axis=-1, keepdims=True))
    decay = jnp.exp(big_g + m - m_new)
    ws = jnp.exp(big_g + r_row - m_new)
    st_new = decay * st + _dot((vext.astype(F32) * ws).astype(BF16), k)
    return h_t.T, st_new, m_new


def _mlstm_kernel(k_ref, qt_ref, vt_ref, gcol_ref, grow_ref, so_ref, xc_ref, zs_ref, ng_ref, skip_ref,
                  out_ref, stf_ref, stb_ref, hf_ref, hb_ref, *, seq_len):
    L = MLSTM_CHUNK
    nc = seq_len // L
    U = MLSTM_UNROLL
    stf_ref[...] = jnp.zeros_like(stf_ref)
    stb_ref[...] = jnp.zeros_like(stb_ref)
    row = lax.broadcasted_iota(jnp.int32, (L, L), 0)
    col = lax.broadcasted_iota(jnp.int32, (L, L), 1)
    causal = row <= col
    anti = row >= col

    def load(o, lane, grow_row):
        return (k_ref[pl.ds(o, L), :], qt_ref[:, pl.ds(o, L)], vt_ref[:, pl.ds(o, L)],
                gcol_ref[pl.ds(o, L), lane:lane + 1], grow_ref[grow_row:grow_row + 1, pl.ds(o, L)],
                grow_ref[grow_row + 8:grow_row + 9, pl.ds(o, L)],
                grow_ref[grow_row + 16:grow_row + 17, pl.ds(o, L)])

    def body(j, carry):
        m_f, m_b = carry
        offs_f = [pl.multiple_of((j * U + u) * L, L) for u in range(U)]
        offs_b = [pl.multiple_of((nc - 1 - j * U - u) * L, L) for u in range(U)]
        ins_f = [load(o, 0, 0) for o in offs_f]
        ins_b = [load(o, 4, 4) for o in offs_b]
        st_f = stf_ref[...]
        st_b = stb_ref[...]
        hs_f, hs_b = [], []
        for u in range(U):
            h, st_f, m_f = _mlstm_chunk(*ins_f[u], st_f, m_f, causal)
            hs_f.append(h)
            h, st_b, m_b = _mlstm_chunk(*ins_b[u], st_b, m_b, anti)
            hs_b.append(h)
        for u in range(U):
            hf_ref[pl.ds(offs_f[u], L), :] = hs_f[u]
            hb_ref[pl.ds(offs_b[u], L), :] = hs_b[u]
        stf_ref[...] = st_f
        stb_ref[...] = st_b
        return m_f, m_b

    m0 = jnp.full((1, 1), M_INIT, F32)
    lax.fori_loop(0, nc // U, body, (m0, m0))

    rows = 512

    def fin(j, _):
        o = pl.multiple_of(j * rows, rows)
        sl = pl.ds(o, rows)
        h = (hf_ref[sl, :] + hb_ref[sl, :]) * so_ref[sl, :].astype(F32)
        mu = jnp.mean(h, axis=-1, keepdims=True)
        d = h - mu
        var = jnp.mean(d * d, axis=-1, keepdims=True)
        hn = d * lax.rsqrt(var + LN_EPS) * ng_ref[...]
        out = (hn + skip_ref[...] * xc_ref[sl, :].astype(F32)) * zs_ref[sl, :].astype(F32)
        out_ref[sl, :] = out.astype(BF16)
        return 0

    lax.fori_loop(0, seq_len // rows, fin, 0)


def _mlstm(k, qt, vt, gcol, grow, so, xc, zs, norm_g, skip, batch, seq_len):
    n = k.shape[0]
    H = MLSTM_HEADS
    dh = MLSTM_HEAD_DIM
    tok = pl.BlockSpec((seq_len, dh), lambda b, h: (b, h))
    vec = pl.BlockSpec((1, dh), lambda b, h: (0, h))
    return pl.pallas_call(
        functools.partial(_mlstm_kernel, seq_len=seq_len),
        grid=(batch, H),
        in_specs=[tok, pl.BlockSpec((dh, seq_len), lambda b, h: (b * H + h, 0)),
                  pl.BlockSpec((dh, seq_len), lambda b, h: (h, b)),
                  tok, pl.BlockSpec((GATE_ROWS, seq_len), lambda b, h: (h, b)),
                  tok, tok, tok, vec, vec],
        out_specs=tok,
        out_shape=jax.ShapeDtypeStruct((n, H * dh), BF16),
        scratch_shapes=[pltpu.VMEM((2 * LANES, dh), F32), pltpu.VMEM((2 * LANES, dh), F32),
                        pltpu.VMEM((seq_len, dh), F32), pltpu.VMEM((seq_len, dh), F32)],
        compiler_params=_params(("parallel", "parallel")),
        name="mlstm",
    )(k, qt, vt, gcol, grow, so, xc, zs, norm_g, skip)


def _outproj_ln_kernel(a_ref, b_ref, wa_ref, wb_ref, x_ref, g_ref, beta_ref, y_ref):
    mix = _dot(a_ref[...], wa_ref[...]) + _dot(b_ref[...], wb_ref[...])
    z = ALPHA * x_ref[...] + mix
    mu = jnp.mean(z, axis=-1, keepdims=True)
    d = z - mu
    var = jnp.mean(d * d, axis=-1, keepdims=True)
    y_ref[...] = d * lax.rsqrt(var + LN_EPS) * g_ref[...] + beta_ref[...]


def _outproj_ln(a, a_blk, b, b_blk, w_a, w_b, x2d, ln_g, ln_b, tm):
    n = x2d.shape[0]
    half = w_a.shape[0]
    full = lambda arr: pl.BlockSpec(arr.shape, lambda i: (0,) * arr.ndim)
    return pl.pallas_call(
        _outproj_ln_kernel,
        grid=(n // tm,),
        in_specs=[pl.BlockSpec((tm, half), lambda i: (i, a_blk)),
                  pl.BlockSpec((tm, half), lambda i: (i, b_blk)),
                  full(w_a), full(w_b),
                  pl.BlockSpec((tm, D_MODEL), lambda i: (i, 0)), full(ln_g), full(ln_b)],
        out_specs=pl.BlockSpec((tm, D_MODEL), lambda i: (i, 0)),
        out_shape=jax.ShapeDtypeStruct((n, D_MODEL), F32),
        compiler_params=_params(("parallel",)),
        name="outproj_ln",
    )(a, b, w_a, w_b, x2d, ln_g, ln_b)


def _rope(xn, cos, sin_signed, first_half):
    partner = jnp.where(first_half, pltpu.roll(xn, LANES - AXIS_DIM // 2, axis=1),
                        pltpu.roll(xn, AXIS_DIM // 2, axis=1))
    return xn * cos + partner * sin_signed


def _inproj1_kernel(x_ref, w_ref, cos_ref, sin_ref, qg_ref, kg_ref, q_ref, k_ref, vext_ref, zs_ref):
    x = x_ref[...].astype(BF16)
    cos = cos_ref[...]
    sin = sin_ref[...]
    tm = x.shape[0]
    lane = lax.broadcasted_iota(jnp.int32, (tm, LANES), 1)
    first_half = (lane % AXIS_DIM) < (AXIS_DIM // 2)
    dh = HEAD_DIM

    def norm_rope(y, gain, scale):
        ms = jnp.mean(y * y, axis=-1, keepdims=True)
        yn = y * lax.rsqrt(ms + RMS_EPS) * gain
        out = _rope(yn, cos, sin, first_half)
        return out * scale if scale is not None else out

    for s in range(ATTN_WIDTH // 512):
        y = _dot(x, w_ref[:, s * 512:(s + 1) * 512])
        for h in range(512 // dh):
            cols = slice(s * 512 + h * dh, s * 512 + (h + 1) * dh)
            q_ref[:, cols] = norm_rope(y[:, h * dh:(h + 1) * dh], qg_ref[...], dh ** -0.5 * LOG2E).astype(BF16)
    kv = _dot(x, w_ref[:, ATTN_WIDTH:ATTN_WIDTH + 2 * KV_WIDTH])
    one_col = (lane == 0).astype(BF16)
    for h in range(KV_HEADS):
        k_ref[:, h * dh:(h + 1) * dh] = norm_rope(kv[:, h * dh:(h + 1) * dh], kg_ref[...], None).astype(BF16)
        vext_ref[:, 2 * h * dh:(2 * h + 1) * dh] = kv[:, KV_WIDTH + h * dh:KV_WIDTH + (h + 1) * dh].astype(BF16)
        vext_ref[:, (2 * h + 1) * dh:(2 * h + 2) * dh] = one_col
    z0 = ATTN_WIDTH + 2 * KV_WIDTH
    for s in range(ATTN_WIDTH // 512):
        zs_ref[:, s * 512:(s + 1) * 512] = _silu(_dot(x, w_ref[:, z0 + s * 512:z0 + (s + 1) * 512])).astype(BF16)


def _inproj1(x2d, w, cos, sin, q_gain, k_gain, seq_len, tm):
    n = x2d.shape[0]
    nt = seq_len // tm
    tok = lambda width: pl.BlockSpec((tm, width), lambda i: (i, 0))
    pos = pl.BlockSpec((tm, LANES), lambda i: (i % nt, 0))
    full = lambda a: pl.BlockSpec(a.shape, lambda i: (0,) * a.ndim)
    return pl.pallas_call(
        _inproj1_kernel,
        grid=(n // tm,),
        in_specs=[tok(D_MODEL), full(w), pos, pos, full(q_gain), full(k_gain)],
        out_specs=[tok(ATTN_WIDTH), tok(KV_WIDTH), tok(2 * KV_WIDTH), tok(ATTN_WIDTH)],
        out_shape=[
            jax.ShapeDtypeStruct((n, ATTN_WIDTH), BF16),
            jax.ShapeDtypeStruct((n, KV_WIDTH), BF16),
            jax.ShapeDtypeStruct((n, 2 * KV_WIDTH), BF16),
            jax.ShapeDtypeStruct((n, ATTN_WIDTH), BF16),
        ],
        compiler_params=_params(("parallel",)),
        name="inproj_odd",
    )(x2d, w, cos, sin, q_gain, k_gain)


def _attn_kernel(q_ref, k_ref, vext_ref, zs_ref, o_ref, s_ref, mrun_ref, mrep_ref, acc_ref, *, seq_len, tk):
    dh = HEAD_DIM
    nk = seq_len // tk
    nl = tk // LANES

    def qk_chunk(h, slot, j):
        o = pl.multiple_of(j * tk, tk)
        s = _dot_nt(q_ref[:, h * dh:(h + 1) * dh], k_ref[pl.ds(o, tk), :])
        s_ref[slot, :, pl.ds(o, tk)] = s
        m = mrun_ref[...]
        for c in range(nl):
            m = jnp.maximum(m, s[:, c * LANES:(c + 1) * LANES])
        mrun_ref[...] = m

    def finish_max(slot):
        m = jnp.max(mrun_ref[...], axis=-1, keepdims=True)
        mrep_ref[slot] = jnp.broadcast_to(m, mrun_ref.shape)
        mrun_ref[...] = jnp.full_like(mrun_ref, -jnp.inf)

    def pv_chunk(slot, j):
        o = pl.multiple_of(j * tk, tk)
        m = mrep_ref[slot]
        s = s_ref[slot, :, pl.ds(o, tk)]
        p = jnp.concatenate([jnp.exp2(s[:, c * LANES:(c + 1) * LANES] - m) for c in range(nl)], axis=1)
        acc_ref[...] += _dot(p.astype(BF16), vext_ref[pl.ds(o, tk), :])

    mrun_ref[...] = jnp.full_like(mrun_ref, -jnp.inf)

    def first(j, _):
        qk_chunk(0, 0, j)
        return 0

    lax.fori_loop(0, nk, first, 0)
    finish_max(0)
    for h in range(GQA_GROUP):
        slot = h % 2
        acc_ref[...] = jnp.zeros_like(acc_ref)
        if h + 1 < GQA_GROUP:
            def both(j, _, h=h, slot=slot):
                qk_chunk(h + 1, 1 - slot, j)
                pv_chunk(slot, j)
                return 0

            lax.fori_loop(0, nk, both, 0)
            finish_max(1 - slot)
        else:
            def last(j, _, slot=slot):
                pv_chunk(slot, j)
                return 0

            lax.fori_loop(0, nk, last, 0)
        acc = acc_ref[...]
        out = acc[:, :dh] / acc[:, dh:dh + 1]
        o_ref[:, h * dh:(h + 1) * dh] = (out * zs_ref[:, h * dh:(h + 1) * dh].astype(F32)).astype(BF16)


def _attention(q, k, vext, zs, batch, seq_len, tq, tk):
    n = q.shape[0]
    dh = HEAD_DIM
    nq = seq_len // tq
    gw = GQA_GROUP * dh
    qspec = pl.BlockSpec((tq, gw), lambda b, g, i: (b * nq + i, g))
    return pl.pallas_call(
        functools.partial(_attn_kernel, seq_len=seq_len, tk=tk),
        grid=(batch, KV_HEADS, nq),
        in_specs=[qspec,
                  pl.BlockSpec((seq_len, dh), lambda b, g, i: (b, g)),
                  pl.BlockSpec((seq_len, 2 * dh), lambda b, g, i: (b, g)),
                  qspec],
        out_specs=qspec,
        out_shape=jax.ShapeDtypeStruct((n, ATTN_WIDTH), BF16),
        scratch_shapes=[pltpu.VMEM((2, tq, seq_len), F32),
                        pltpu.VMEM((tq, LANES), F32),
                        pltpu.VMEM((2, tq, LANES), F32),
                        pltpu.VMEM((tq, 2 * dh), F32)],
        compiler_params=_params(("parallel", "parallel", "parallel")),
        name="gqa_attn",
    )(q, k, vext, zs)


def _rope_tables(seq_len):
    t = np.arange(seq_len)
    n_freq = AXIS_DIM // 2
    inv = jnp.asarray(ROPE_THETA, F32) ** (-jnp.arange(n_freq, dtype=F32) / n_freq)
    t_row = jnp.asarray(t // GRID_W, F32)
    t_col = jnp.asarray(t % GRID_W, F32)
    ang = jnp.stack([t_row[:, None] * inv, t_col[:, None] * inv], axis=1)
    cos = jnp.cos(ang)
    sin = jnp.sin(ang)
    cos_full = jnp.concatenate([cos, cos], axis=-1).reshape(seq_len, HEAD_DIM)
    sin_signed = jnp.concatenate([-sin, sin], axis=-1).reshape(seq_len, HEAD_DIM)
    return cos_full, sin_signed


def _chunk_matrices(tm):
    idx = np.arange(tm)
    same = (idx[:, None] // MLSTM_CHUNK) == (idx[None, :] // MLSTM_CHUNK)
    tri = same & (idx[None, :] <= idx[:, None])
    return jnp.asarray(tri, BF16), jnp.asarray(same, BF16)


def _pad_lanes(a):
    return jnp.pad(a, ((0, 0), (0, LANES - a.shape[-1])))


def _even_layer(x2d, batch, seq_len, p, j, layer):
    W = POOL_WIDTH
    tm = 512
    w_in = p['w_in_even'][j]
    w_main = jnp.concatenate([w_in[:, :3 * W], w_in[:, 4 * W:6 * W]], axis=1).astype(BF16)
    w_vt = w_in[:, 3 * W:4 * W].T.astype(BF16)
    w_gate = _pad_lanes(w_in[:, 6 * W:]).astype(BF16)
    b_gate = jnp.stack([_pad_lanes(p['b_gate_i'][j].reshape(1, -1))[0],
                        _pad_lanes(p['b_gate_f'][j].reshape(1, -1))[0]]).astype(F32)
    tri, ones_bd = _chunk_matrices(tm)
    xa, zas, xb, vt, so, zs, gcol, grow = _inproj0(x2d, w_main, w_vt, w_gate, b_gate, tri, ones_bd, tm)
    out_a, k, qt, xc = _mix0(
        xa, xb, zas, p['w_pool'][j].astype(BF16), p['pool_scale'][j].reshape(1, W),
        p['conv_w'][j], p['conv_b'][j].reshape(1, W), p['w_k_m'][j].astype(BF16),
        jnp.swapaxes(p['w_q_m'][j], 1, 2).astype(BF16), batch, seq_len, tm)
    out_b = _mlstm(k, qt, vt, gcol, grow, so, xc, zs, p['mh_norm_g'][j].reshape(1, W),
                   p['skip'][j].reshape(1, W), batch, seq_len)
    w_out = p['w_out_even'][j].astype(BF16)
    return _outproj_ln(out_a, 0, out_b, 0, w_out[:W], w_out[W:], x2d,
                       p['ln_g'][layer].reshape(1, -1), p['ln_b'][layer].reshape(1, -1), 1024)


def _odd_layer(x2d, batch, seq_len, p, j, layer):
    cos, sin = _rope_tables(seq_len)
    q, k, vext, zs = _inproj1(x2d, p['w_in_odd'][j].astype(BF16), cos, sin,
                              p['q_norm_g'][j].reshape(1, -1), p['k_norm_g'][j].reshape(1, -1), seq_len, 512)
    o = _attention(q, k, vext, zs, batch, seq_len, 512, min(2048, seq_len))
    w_out = p['w_out_odd'][j].astype(BF16)
    half = ATTN_WIDTH // 2
    return _outproj_ln(o, 0, o, 1, w_out[:half], w_out[half:], x2d,
                       p['ln_g'][layer].reshape(1, -1), p['ln_b'][layer].reshape(1, -1), 1024)


def _trunk(x, p):
    batch, seq_len, _ = x.shape
    x2d = x.reshape(batch * seq_len, D_MODEL)
    for layer in range(DEPTH):
        j = layer // 2
        if layer % 2 == 0:
            x2d = _even_layer(x2d, batch, seq_len, p, j, layer)
        else:
            x2d = _odd_layer(x2d, batch, seq_len, p, j, layer)
    return x2d.reshape(batch, seq_len, D_MODEL)


def kernel(x_prompt, x_sample, w_in_even, w_pool, pool_scale, conv_w, conv_b, w_q_m, w_k_m, b_gate_i, b_gate_f, mh_norm_g, skip, w_out_even, w_in_odd, q_norm_g, k_norm_g, w_out_odd, ln_g, ln_b):
    p = dict(w_in_even=w_in_even, w_pool=w_pool, pool_scale=pool_scale, conv_w=conv_w, conv_b=conv_b,
             w_q_m=w_q_m, w_k_m=w_k_m, b_gate_i=b_gate_i, b_gate_f=b_gate_f, mh_norm_g=mh_norm_g,
             skip=skip, w_out_even=w_out_even, w_in_odd=w_in_odd, q_norm_g=q_norm_g, k_norm_g=k_norm_g,
             w_out_odd=w_out_odd, ln_g=ln_g, ln_b=ln_b)
    return _trunk(x_prompt, p), _trunk(x_sample, p)
```

```python
import functools

import jax
import jax.numpy as jnp
import numpy as np
from jax import lax
from jax.experimental import pallas as pl
from jax.experimental.pallas import tpu as pltpu

D_MODEL = 1024
GRID_W = 64
POOL_WIDTH = 512
POOL_GROUPS = 4
POOL_GROUP_DIM = POOL_WIDTH // POOL_GROUPS
POOL_WINDOWS = (2, 4, 8, 16)
MLSTM_HEADS = 4
MLSTM_HEAD_DIM = 128
MLSTM_WIDTH = MLSTM_HEADS * MLSTM_HEAD_DIM
MLSTM_CHUNK = 256
N_DIRS = 2
ATTN_HEADS = 8
KV_HEADS = 2
HEAD_DIM = 128
ATTN_WIDTH = ATTN_HEADS * HEAD_DIM
KV_WIDTH = KV_HEADS * HEAD_DIM
GQA_GROUP = ATTN_HEADS // KV_HEADS
ROPE_THETA = 10000.0
AXIS_DIM = HEAD_DIM // 2
DEPTH = 2
ALPHA = (2 * DEPTH) ** 0.25
LN_EPS = 1e-5
RMS_EPS = 1e-6
M_INIT = -1e30
LOG2E = 1.4426950408889634

LANES = 128
HALO = 8
GATE_ROWS = 8
MLSTM_UNROLL = 4
VMEM_LIMIT = 56 * 1024 * 1024

F32 = jnp.float32
BF16 = jnp.bfloat16


def _dot(a, b):
    return jnp.dot(a, b, preferred_element_type=F32)


def _dot_nt(a, b):
    return lax.dot_general(a, b, (((1,), (1,)), ((), ())), preferred_element_type=F32)


def _sigmoid(x):
    return 1.0 / (1.0 + jnp.exp(-x))


def _silu(x):
    return x * _sigmoid(x)


def _log_sigmoid(x):
    return jnp.minimum(x, 0.0) - jnp.log1p(jnp.exp(-jnp.abs(x)))


def _params(semantics):
    return pltpu.CompilerParams(dimension_semantics=semantics, vmem_limit_bytes=VMEM_LIMIT)


def _inproj0_kernel(x_ref, w_ref, wvt_ref, wg_ref, bg_ref, tri_ref, ones_ref,
                    xa_ref, zas_ref, xb_ref, vt_ref, so_ref, zs_ref, gcol_ref, grow_ref):
    x = x_ref[...].astype(BF16)
    W = POOL_WIDTH

    pre = _dot(x, wg_ref[...])
    ig = pre + bg_ref[0:1, :]
    lf = _log_sigmoid(pltpu.roll(pre, LANES - 8, axis=1) + bg_ref[1:2, :])
    hi = lf.astype(BF16)
    lo = (lf - hi.astype(F32)).astype(BF16)
    g = _dot(tri_ref[...], hi) + _dot(tri_ref[...], lo)
    tot = _dot(ones_ref[...], hi) + _dot(ones_ref[...], lo)
    lane = lax.broadcasted_iota(jnp.int32, g.shape, 1)
    gsel = jnp.where(lane < MLSTM_HEADS, g, tot - g + lf)
    r = ig - gsel
    cm = jnp.where(lane < 8, r, jnp.where(lane < 16, pltpu.roll(tot, 8, axis=1), pltpu.roll(gsel, 16, axis=1)))
    tr = cm.T
    H = MLSTM_HEADS
    for h in range(H):
        sh = (LANES - h) % LANES
        gcol_ref[:, h * LANES:(h + 1) * LANES] = pltpu.roll(r, sh, axis=1) if sh else r
        picks = [tr[q * 2 * H + d * H + h:q * 2 * H + d * H + h + 1] for q in range(3) for d in range(N_DIRS)]
        grow_ref[h * GATE_ROWS:(h + 1) * GATE_ROWS, :] = jnp.concatenate(picks + [tr[6 * H:6 * H + 2]], axis=0)

    xa_ref[...] = _dot(x, w_ref[:, 0 * W:1 * W])
    zas_ref[...] = _silu(_dot(x, w_ref[:, 1 * W:2 * W])).astype(BF16)
    xb_ref[...] = _dot(x, w_ref[:, 2 * W:3 * W])
    vt_ref[...] = _dot_nt(wvt_ref[...], x).astype(BF16)
    so_ref[...] = _sigmoid(_dot(x, w_ref[:, 3 * W:4 * W])).astype(BF16)
    zs_ref[...] = _silu(_dot(x, w_ref[:, 4 * W:5 * W])).astype(BF16)


def _inproj0(x2d, w_main, w_vt, w_gate, b_gate, tri, ones_bd, tm):
    n = x2d.shape[0]
    W = POOL_WIDTH
    tok = lambda width: pl.BlockSpec((tm, width), lambda i: (i, 0))
    full = lambda a: pl.BlockSpec(a.shape, lambda i: (0,) * a.ndim)
    return pl.pallas_call(
        _inproj0_kernel,
        grid=(n // tm,),
        in_specs=[tok(D_MODEL), full(w_main), full(w_vt), full(w_gate), full(b_gate), full(tri), full(ones_bd)],
        out_specs=[tok(W), tok(W), tok(W), pl.BlockSpec((W, tm), lambda i: (0, i)), tok(W), tok(W),
                   tok(MLSTM_HEADS * LANES),
                   pl.BlockSpec((MLSTM_HEADS * GATE_ROWS, tm), lambda i: (0, i))],
        out_shape=[
            jax.ShapeDtypeStruct((n, W), F32),
            jax.ShapeDtypeStruct((n, W), BF16),
            jax.ShapeDtypeStruct((n, W), F32),
            jax.ShapeDtypeStruct((W, n), BF16),
            jax.ShapeDtypeStruct((n, W), BF16),
            jax.ShapeDtypeStruct((n, W), BF16),
            jax.ShapeDtypeStruct((n, MLSTM_HEADS * LANES), F32),
            jax.ShapeDtypeStruct((MLSTM_HEADS * GATE_ROWS, n), F32),
        ],
        compiler_params=_params(("parallel",)),
        name="inproj_even",
    )(x2d, w_main, w_vt, w_gate, b_gate, tri, ones_bd)


def _mix0_kernel(xa_ref, xap_ref, xan_ref, xb_ref, xbp_ref, xbn_ref, zas_ref,
                 wpool_ref, pscale_ref, convw_ref, convb_ref, wk_ref, wqt_ref,
                 outa_ref, k_ref, qt_ref, xc_ref, exta_ref, extb_ref, *, seq_len, tm):
    i = pl.program_id(1)
    first = i == 0
    last = i == pl.num_programs(1) - 1
    for ext, main, prev, nxt in ((exta_ref, xa_ref, xap_ref, xan_ref), (extb_ref, xb_ref, xbp_ref, xbn_ref)):
        ext[0:HALO, :] = jnp.where(first, 0.0, prev[...])
        ext[HALO:HALO + tm, :] = main[...]
        ext[HALO + tm:HALO + tm + HALO, :] = jnp.where(last, 0.0, nxt[...])

    t = i * tm + lax.broadcasted_iota(jnp.int32, (tm, 1), 0)
    GD = POOL_GROUP_DIM
    for g, w in enumerate(POOL_WINDOWS):
        left = w // 2
        right = w - 1 - left
        cols = slice(g * GD, (g + 1) * GD)
        acc = exta_ref[HALO - left:HALO - left + tm, cols]
        for off in range(-left + 1, right + 1):
            acc = acc + exta_ref[HALO + off:HALO + off + tm, cols]
        cnt = (jnp.minimum(t + right + 1, seq_len) - jnp.maximum(t - left, 0)).astype(F32)
        pooled = acc / cnt - xa_ref[:, cols]
        mixed = _dot(pooled.astype(BF16), wpool_ref[g])
        outa_ref[:, cols] = (mixed * pscale_ref[:, cols] * zas_ref[:, cols].astype(F32)).astype(BF16)

    conv = (extb_ref[HALO - 1:HALO - 1 + tm, :] * convw_ref[0:1, :]
            + extb_ref[HALO:HALO + tm, :] * convw_ref[1:2, :]
            + extb_ref[HALO + 1:HALO + 1 + tm, :] * convw_ref[2:3, :]
            + convb_ref[...])
    xc = _silu(conv)
    xc_ref[...] = xc.astype(BF16)
    xcb = xc.astype(BF16)
    dh = MLSTM_HEAD_DIM
    for h in range(MLSTM_HEADS):
        cols = slice(h * dh, (h + 1) * dh)
        k_ref[:, cols] = (_dot(xcb[:, cols], wk_ref[h]) * (dh ** -0.5)).astype(BF16)
        qt_ref[cols, :] = _dot_nt(wqt_ref[h], xcb[:, cols]).astype(BF16)


def _mix0(xa, xb, zas, w_pool, pool_scale, conv_w, conv_b, w_k, w_qt, batch, seq_len, tm):
    n = xa.shape[0]
    W = POOL_WIDTH
    nt = seq_len // tm
    hb = tm // HALO
    nhalo = n // HALO
    main = pl.BlockSpec((tm, W), lambda b, i: (b * nt + i, 0))
    prev = pl.BlockSpec((HALO, W), lambda b, i: (jnp.maximum((b * nt + i) * hb - 1, 0), 0))
    nxt = pl.BlockSpec((HALO, W), lambda b, i: (jnp.minimum((b * nt + i + 1) * hb, nhalo - 1), 0))
    full = lambda a: pl.BlockSpec(a.shape, lambda b, i: (0,) * a.ndim)
    return pl.pallas_call(
        functools.partial(_mix0_kernel, seq_len=seq_len, tm=tm),
        grid=(batch, nt),
        in_specs=[main, prev, nxt, main, prev, nxt, main,
                  full(w_pool), full(pool_scale), full(conv_w), full(conv_b), full(w_k), full(w_qt)],
        out_specs=[main, main, pl.BlockSpec((W, tm), lambda b, i: (b, i)), main],
        out_shape=[
            jax.ShapeDtypeStruct((n, W), BF16),
            jax.ShapeDtypeStruct((n, W), BF16),
            jax.ShapeDtypeStruct((batch * W, seq_len), BF16),
            jax.ShapeDtypeStruct((n, W), BF16),
        ],
        scratch_shapes=[pltpu.VMEM((tm + 2 * HALO, W), F32), pltpu.VMEM((tm + 2 * HALO, W), F32)],
        compiler_params=_params(("parallel", "parallel")),
        name="mix_even",
    )(xa, xa, xa, xb, xb, xb, zas, w_pool, pool_scale, conv_w, conv_b, w_k, w_qt)


def _mlstm_chunk(k, qt, vt, r_col, r_row, tot_row, g_row, st, m, mask):
    L = MLSTM_CHUNK
    one_row = (lax.broadcasted_iota(jnp.int32, (LANES, L), 0) == 0).astype(BF16)
    vext = jnp.concatenate([vt, one_row], axis=0)
    both = _dot(jnp.concatenate([k, st.astype(BF16)], axis=0), qt)
    kq = both[:L]
    inter = both[L:]
    dm = jnp.where(mask, r_col + g_row, -jnp.inf)
    inter_m = g_row + m
    m_row = jnp.maximum(inter_m, jnp.max(dm, axis=0, keepdims=True))
    p = kq * jnp.exp(dm - m_row)
    intra = _dot(vext, p.astype(BF16))
    ex = jnp.exp(inter_m - m_row)
    num = intra[:LANES] + ex * inter[:LANES]
    den = intra[LANES:LANES + 1] + ex * inter[LANES:LANES + 1]
    h_t = num / jnp.maximum(jnp.abs(den), jnp.exp(-m_row))
    big_g = tot_row[:, 0:1]
    m_new = big_g + jnp.maximum(m, jnp.max(r_row, axis=-1, keepdims=True))
    decay = jnp.exp(big_g + m - m_new)
    ws = jnp.exp(big_g + r_row - m_new)
    st_new = decay * st + _dot((vext.astype(F32) * ws).astype(BF16), k)
    return h_t.T, st_new, m_new


def _mlstm_kernel(k_ref, qt_ref, vt_ref, gcol_ref, grow_ref, so_ref, xc_ref, zs_ref, ng_ref, skip_ref,
                  out_ref, stf_ref, stb_ref, hf_ref, hb_ref, *, seq_len):
    L = MLSTM_CHUNK
    nc = seq_len // L
    U = MLSTM_UNROLL
    stf_ref[...] = jnp.zeros_like(stf_ref)
    stb_ref[...] = jnp.zeros_like(stb_ref)
    row = lax.broadcasted_iota(jnp.int32, (L, L), 0)
    col = lax.broadcasted_iota(jnp.int32, (L, L), 1)
    causal = row <= col
    anti = row >= col

    def load(o, d):
        lane = d * MLSTM_HEADS
        return (k_ref[pl.ds(o, L), :], qt_ref[:, pl.ds(o, L)], vt_ref[:, pl.ds(o, L)],
                gcol_ref[pl.ds(o, L), lane:lane + 1], grow_ref[d:d + 1, pl.ds(o, L)],
                grow_ref[2 + d:3 + d, pl.ds(o, L)], grow_ref[4 + d:5 + d, pl.ds(o, L)])

    def body(j, carry):
        m_f, m_b = carry
        offs_f = [pl.multiple_of((j * U + u) * L, L) for u in range(U)]
        offs_b = [pl.multiple_of((nc - 1 - j * U - u) * L, L) for u in range(U)]
        ins_f = [load(o, 0) for o in offs_f]
        ins_b = [load(o, 1) for o in offs_b]
        st_f = stf_ref[...]
        st_b = stb_ref[...]
        hs_f, hs_b = [], []
        for u in range(U):
            h, st_f, m_f = _mlstm_chunk(*ins_f[u], st_f, m_f, causal)
            hs_f.append(h)
            h, st_b, m_b = _mlstm_chunk(*ins_b[u], st_b, m_b, anti)
            hs_b.append(h)
        for u in range(U):
            hf_ref[pl.ds(offs_f[u], L), :] = hs_f[u]
            hb_ref[pl.ds(offs_b[u], L), :] = hs_b[u]
        stf_ref[...] = st_f
        stb_ref[...] = st_b
        return m_f, m_b

    m0 = jnp.full((1, 1), M_INIT, F32)
    lax.fori_loop(0, nc // U, body, (m0, m0))

    rows = 512

    def fin(j, _):
        o = pl.multiple_of(j * rows, rows)
        sl = pl.ds(o, rows)
        h = (hf_ref[sl, :] + hb_ref[sl, :]) * so_ref[sl, :].astype(F32)
        mu = jnp.mean(h, axis=-1, keepdims=True)
        d = h - mu
        var = jnp.mean(d * d, axis=-1, keepdims=True)
        hn = d * lax.rsqrt(var + LN_EPS) * ng_ref[...]
        out = (hn + skip_ref[...] * xc_ref[sl, :].astype(F32)) * zs_ref[sl, :].astype(F32)
        out_ref[sl, :] = out.astype(BF16)
        return 0

    lax.fori_loop(0, seq_len // rows, fin, 0)


def _mlstm(k, qt, vt, gcol, grow, so, xc, zs, norm_g, skip, batch, seq_len):
    n = k.shape[0]
    H = MLSTM_HEADS
    dh = MLSTM_HEAD_DIM
    tok = pl.BlockSpec((seq_len, dh), lambda b, h: (b, h))
    vec = pl.BlockSpec((1, dh), lambda b, h: (0, h))
    return pl.pallas_call(
        functools.partial(_mlstm_kernel, seq_len=seq_len),
        grid=(batch, H),
        in_specs=[tok, pl.BlockSpec((dh, seq_len), lambda b, h: (b * H + h, 0)),
                  pl.BlockSpec((dh, seq_len), lambda b, h: (h, b)),
                  tok, pl.BlockSpec((GATE_ROWS, seq_len), lambda b, h: (h, b)),
                  tok, tok, tok, vec, vec],
        out_specs=tok,
        out_shape=jax.ShapeDtypeStruct((n, H * dh), BF16),
        scratch_shapes=[pltpu.VMEM((2 * LANES, dh), F32), pltpu.VMEM((2 * LANES, dh), F32),
                        pltpu.VMEM((seq_len, dh), F32), pltpu.VMEM((seq_len, dh), F32)],
        compiler_params=_params(("parallel", "parallel")),
        name="mlstm",
    )(k, qt, vt, gcol, grow, so, xc, zs, norm_g, skip)


def _outproj_ln_kernel(a_ref, b_ref, wa_ref, wb_ref, x_ref, g_ref, beta_ref, y_ref):
    mix = _dot(a_ref[...], wa_ref[...]) + _dot(b_ref[...], wb_ref[...])
    z = ALPHA * x_ref[...] + mix
    mu = jnp.mean(z, axis=-1, keepdims=True)
    d = z - mu
    var = jnp.mean(d * d, axis=-1, keepdims=True)
    y_ref[...] = d * lax.rsqrt(var + LN_EPS) * g_ref[...] + beta_ref[...]


def _outproj_ln(a, a_blk, b, b_blk, w_a, w_b, x2d, ln_g, ln_b, tm):
    n = x2d.shape[0]
    half = w_a.shape[0]
    full = lambda arr: pl.BlockSpec(arr.shape, lambda i: (0,) * arr.ndim)
    return pl.pallas_call(
        _outproj_ln_kernel,
        grid=(n // tm,),
        in_specs=[pl.BlockSpec((tm, half), lambda i: (i, a_blk)),
                  pl.BlockSpec((tm, half), lambda i: (i, b_blk)),
                  full(w_a), full(w_b),
                  pl.BlockSpec((tm, D_MODEL), lambda i: (i, 0)), full(ln_g), full(ln_b)],
        out_specs=pl.BlockSpec((tm, D_MODEL), lambda i: (i, 0)),
        out_shape=jax.ShapeDtypeStruct((n, D_MODEL), F32),
        compiler_params=_params(("parallel",)),
        name="outproj_ln",
    )(a, b, w_a, w_b, x2d, ln_g, ln_b)


def _rope(xn, cos, sin_signed):
    return xn * cos + pltpu.roll(xn, LANES // 2, axis=1) * sin_signed


def _inproj1_kernel(x_ref, w_ref, cos_ref, sin_ref, qg_ref, kg_ref, q_ref, k_ref, vext_ref, zs_ref):
    x = x_ref[...].astype(BF16)
    cos = cos_ref[...]
    sin = sin_ref[...]
    tm = x.shape[0]
    dh = HEAD_DIM
    SW = 4 * dh

    def norm_rope(y, gain, scale):
        ms = jnp.mean(y * y, axis=-1, keepdims=True)
        yn = y * lax.rsqrt(ms + RMS_EPS) * gain
        out = _rope(yn, cos, sin)
        return out * scale if scale is not None else out

    def q_epilogue(y, s):
        for h in range(SW // dh):
            cols = slice(s * SW + h * dh, s * SW + (h + 1) * dh)
            q_ref[:, cols] = norm_rope(y[:, h * dh:(h + 1) * dh], qg_ref[...], dh ** -0.5 * LOG2E).astype(BF16)

    def kv_epilogue(y, s):
        one_col = (lax.broadcasted_iota(jnp.int32, (tm, LANES), 1) == 0).astype(BF16)
        for h in range(KV_HEADS):
            k_ref[:, h * dh:(h + 1) * dh] = norm_rope(y[:, h * dh:(h + 1) * dh], kg_ref[...], None).astype(BF16)
            vext_ref[:, 2 * h * dh:(2 * h + 1) * dh] = y[:, KV_WIDTH + h * dh:KV_WIDTH + (h + 1) * dh].astype(BF16)
            vext_ref[:, (2 * h + 1) * dh:(2 * h + 2) * dh] = one_col

    def z_epilogue(y, s):
        zs_ref[:, s * SW:(s + 1) * SW] = _silu(y).astype(BF16)

    nq = ATTN_WIDTH // SW
    z0 = ATTN_WIDTH + 2 * KV_WIDTH
    slabs = [(ATTN_WIDTH, kv_epilogue, 0)]
    for s in range(nq):
        slabs += [(s * SW, q_epilogue, s), (z0 + s * SW, z_epilogue, s)]
    y_next = _dot(x, w_ref[:, slabs[0][0]:slabs[0][0] + SW])
    for i, (_, epilogue, s) in enumerate(slabs):
        y = y_next
        if i + 1 < len(slabs):
            c = slabs[i + 1][0]
            y_next = _dot(x, w_ref[:, c:c + SW])
        epilogue(y, s)


def _inproj1(x2d, w, cos, sin, q_gain, k_gain, seq_len, tm):
    n = x2d.shape[0]
    nt = seq_len // tm
    tok = lambda width: pl.BlockSpec((tm, width), lambda i: (i, 0))
    pos = pl.BlockSpec((tm, LANES), lambda i: (i % nt, 0))
    full = lambda a: pl.BlockSpec(a.shape, lambda i: (0,) * a.ndim)
    return pl.pallas_call(
        _inproj1_kernel,
        grid=(n // tm,),
        in_specs=[tok(D_MODEL), full(w), pos, pos, full(q_gain), full(k_gain)],
        out_specs=[tok(ATTN_WIDTH), tok(KV_WIDTH), tok(2 * KV_WIDTH), tok(ATTN_WIDTH)],
        out_shape=[
            jax.ShapeDtypeStruct((n, ATTN_WIDTH), BF16),
            jax.ShapeDtypeStruct((n, KV_WIDTH), BF16),
            jax.ShapeDtypeStruct((n, 2 * KV_WIDTH), BF16),
            jax.ShapeDtypeStruct((n, ATTN_WIDTH), BF16),
        ],
        compiler_params=_params(("parallel",)),
        name="inproj_odd",
    )(x2d, w, cos, sin, q_gain, k_gain)


def _attn_kernel(q_ref, k_ref, vext_ref, zs_ref, o_ref, s_ref, mrun_ref, mrep_ref, *, seq_len, tk):
    dh = HEAD_DIM
    nk = seq_len // tk
    nl = tk // LANES

    def qk_chunk(h, slot, j):
        s = _dot_nt(q_ref[:, h * dh:(h + 1) * dh], k_ref[j * tk:(j + 1) * tk, :])
        s_ref[slot, :, j * tk:(j + 1) * tk] = s
        m = mrun_ref[...]
        for c in range(nl):
            m = jnp.maximum(m, s[:, c * LANES:(c + 1) * LANES])
        mrun_ref[...] = m

    def finish_max(slot):
        m = jnp.max(mrun_ref[...], axis=-1, keepdims=True)
        mrep_ref[slot] = jnp.broadcast_to(m, mrun_ref.shape)
        mrun_ref[...] = jnp.full_like(mrun_ref, -jnp.inf)

    def pv_chunk(slot, j):
        m = mrep_ref[slot]
        s = s_ref[slot, :, j * tk:(j + 1) * tk]
        p = jnp.concatenate([jnp.exp2(s[:, c * LANES:(c + 1) * LANES] - m) for c in range(nl)], axis=1)
        return _dot(p.astype(BF16), vext_ref[j * tk:(j + 1) * tk, :])

    mrun_ref[...] = jnp.full_like(mrun_ref, -jnp.inf)
    for j in range(nk):
        qk_chunk(0, 0, j)
    finish_max(0)
    for h in range(GQA_GROUP):
        slot = h % 2
        acc = None
        for j in range(nk):
            if h + 1 < GQA_GROUP:
                qk_chunk(h + 1, 1 - slot, j)
            part = pv_chunk(slot, j)
            acc = part if acc is None else acc + part
        if h + 1 < GQA_GROUP:
            finish_max(1 - slot)
        out = acc[:, :dh] / acc[:, dh:dh + 1]
        o_ref[:, h * dh:(h + 1) * dh] = (out * zs_ref[:, h * dh:(h + 1) * dh].astype(F32)).astype(BF16)


def _attention(q, k, vext, zs, batch, seq_len, tq, tk):
    n = q.shape[0]
    dh = HEAD_DIM
    nq = seq_len // tq
    gw = GQA_GROUP * dh
    qspec = pl.BlockSpec((tq, gw), lambda b, g, i: (b * nq + i, g))
    return pl.pallas_call(
        functools.partial(_attn_kernel, seq_len=seq_len, tk=tk),
        grid=(batch, KV_HEADS, nq),
        in_specs=[qspec,
                  pl.BlockSpec((seq_len, dh), lambda b, g, i: (b, g)),
                  pl.BlockSpec((seq_len, 2 * dh), lambda b, g, i: (b, g)),
                  qspec],
        out_specs=qspec,
        out_shape=jax.ShapeDtypeStruct((n, ATTN_WIDTH), BF16),
        scratch_shapes=[pltpu.VMEM((2, tq, seq_len), F32),
                        pltpu.VMEM((tq, LANES), F32),
                        pltpu.VMEM((2, tq, LANES), F32)],
        compiler_params=_params(("parallel", "parallel", "parallel")),
        name="gqa_attn",
    )(q, k, vext, zs)


def _rope_perm():
    f = AXIS_DIM // 2
    return np.concatenate([np.arange(0, f), np.arange(2 * f, 3 * f), np.arange(f, 2 * f), np.arange(3 * f, 4 * f)])


def _rope_tables(seq_len):
    t = np.arange(seq_len)
    n_freq = AXIS_DIM // 2
    inv = jnp.asarray(ROPE_THETA, F32) ** (-jnp.arange(n_freq, dtype=F32) / n_freq)
    t_row = jnp.asarray(t // GRID_W, F32)
    t_col = jnp.asarray(t % GRID_W, F32)
    ang = jnp.concatenate([t_row[:, None] * inv, t_col[:, None] * inv], axis=1)
    cos = jnp.cos(ang)
    sin = jnp.sin(ang)
    return jnp.concatenate([cos, cos], axis=1), jnp.concatenate([-sin, sin], axis=1)


def _permute_heads(w, n_heads):
    idx = np.concatenate([h * HEAD_DIM + _rope_perm() for h in range(n_heads)])
    return w[..., idx]


def _chunk_matrices(tm):
    idx = np.arange(tm)
    same = (idx[:, None] // MLSTM_CHUNK) == (idx[None, :] // MLSTM_CHUNK)
    tri = same & (idx[None, :] <= idx[:, None])
    return jnp.asarray(tri, BF16), jnp.asarray(same, BF16)


def _pad_lanes(a):
    return jnp.pad(a, ((0, 0), (0, LANES - a.shape[-1])))


def _even_layer(x2d, batch, seq_len, p, j, layer):
    W = POOL_WIDTH
    tm = 512
    w_in = p['w_in_even'][j]
    w_main = jnp.concatenate([w_in[:, :3 * W], w_in[:, 4 * W:6 * W]], axis=1).astype(BF16)
    w_vt = w_in[:, 3 * W:4 * W].T.astype(BF16)
    w_gate = _pad_lanes(w_in[:, 6 * W:]).astype(BF16)
    b_gate = jnp.stack([_pad_lanes(p['b_gate_i'][j].reshape(1, -1))[0],
                        _pad_lanes(p['b_gate_f'][j].reshape(1, -1))[0]]).astype(F32)
    tri, ones_bd = _chunk_matrices(tm)
    xa, zas, xb, vt, so, zs, gcol, grow = _inproj0(x2d, w_main, w_vt, w_gate, b_gate, tri, ones_bd, tm)
    out_a, k, qt, xc = _mix0(
        xa, xb, zas, p['w_pool'][j].astype(BF16), p['pool_scale'][j].reshape(1, W),
        p['conv_w'][j], p['conv_b'][j].reshape(1, W), p['w_k_m'][j].astype(BF16),
        jnp.swapaxes(p['w_q_m'][j], 1, 2).astype(BF16), batch, seq_len, tm)
    out_b = _mlstm(k, qt, vt, gcol, grow, so, xc, zs, p['mh_norm_g'][j].reshape(1, W),
                   p['skip'][j].reshape(1, W), batch, seq_len)
    w_out = p['w_out_even'][j].astype(BF16)
    return _outproj_ln(out_a, 0, out_b, 0, w_out[:W], w_out[W:], x2d,
                       p['ln_g'][layer].reshape(1, -1), p['ln_b'][layer].reshape(1, -1), 1024)


def _odd_layer(x2d, batch, seq_len, p, j, layer):
    cos, sin = _rope_tables(seq_len)
    w_in = p['w_in_odd'][j]
    w = jnp.concatenate([_permute_heads(w_in[:, :ATTN_WIDTH], ATTN_HEADS),
                         _permute_heads(w_in[:, ATTN_WIDTH:ATTN_WIDTH + KV_WIDTH], KV_HEADS),
                         w_in[:, ATTN_WIDTH + KV_WIDTH:]], axis=1).astype(BF16)
    q, k, vext, zs = _inproj1(x2d, w, cos, sin,
                              _permute_heads(p['q_norm_g'][j].reshape(1, -1), 1),
                              _permute_heads(p['k_norm_g'][j].reshape(1, -1), 1), seq_len, 512)
    o = _attention(q, k, vext, zs, batch, seq_len, 512, min(2048, seq_len))
    w_out = p['w_out_odd'][j].astype(BF16)
    half = ATTN_WIDTH // 2
    return _outproj_ln(o, 0, o, 1, w_out[:half], w_out[half:], x2d,
                       p['ln_g'][layer].reshape(1, -1), p['ln_b'][layer].reshape(1, -1), 1024)


def _trunk(x, p):
    batch, seq_len, _ = x.shape
    x2d = x.reshape(batch * seq_len, D_MODEL)
    for layer in range(DEPTH):
        j = layer // 2
        if layer % 2 == 0:
            x2d = _even_layer(x2d, batch, seq_len, p, j, layer)
        else:
            x2d = _odd_layer(x2d, batch, seq_len, p, j, layer)
    return x2d.reshape(batch, seq_len, D_MODEL)


def kernel(x_prompt, x_sample, w_in_even, w_pool, pool_scale, conv_w, conv_b, w_q_m, w_k_m, b_gate_i, b_gate_f, mh_norm_g, skip, w_out_even, w_in_odd, q_norm_g, k_norm_g, w_out_odd, ln_g, ln_b):
    p = dict(w_in_even=w_in_even, w_pool=w_pool, pool_scale=pool_scale, conv_w=conv_w, conv_b=conv_b,
             w_q_m=w_q_m, w_k_m=w_k_m, b_gate_i=b_gate_i, b_gate_f=b_gate_f, mh_norm_g=mh_norm_g,
             skip=skip, w_out_even=w_out_even, w_in_odd=w_in_odd, q_norm_g=q_norm_g, k_norm_g=k_norm_g,
             w_out_odd=w_out_odd, ln_g=ln_g, ln_b=ln_b)
    return _trunk(x_prompt, p), _trunk(x_sample, p)
```

```python
import functools

import jax
import jax.numpy as jnp
import numpy as np
from jax import lax
from jax.experimental import pallas as pl
from jax.experimental.pallas import tpu as pltpu

D_MODEL = 1024
GRID_W = 64
POOL_WIDTH = 512
POOL_GROUPS = 4
POOL_GROUP_DIM = POOL_WIDTH // POOL_GROUPS
POOL_WINDOWS = (2, 4, 8, 16)
MLSTM_HEADS = 4
MLSTM_HEAD_DIM = 128
MLSTM_WIDTH = MLSTM_HEADS * MLSTM_HEAD_DIM
MLSTM_CHUNK = 256
N_DIRS = 2
ATTN_HEADS = 8
KV_HEADS = 2
HEAD_DIM = 128
ATTN_WIDTH = ATTN_HEADS * HEAD_DIM
KV_WIDTH = KV_HEADS * HEAD_DIM
GQA_GROUP = ATTN_HEADS // KV_HEADS
ROPE_THETA = 10000.0
AXIS_DIM = HEAD_DIM // 2
DEPTH = 2
ALPHA = (2 * DEPTH) ** 0.25
LN_EPS = 1e-5
RMS_EPS = 1e-6
M_INIT = -1e30
LOG2E = 1.4426950408889634

LANES = 128
HALO = 8
GATE_ROWS = 8
MLSTM_UNROLL = 8
STATE_ROWS = 2 * MLSTM_HEAD_DIM
VMEM_LIMIT = 56 * 1024 * 1024

F32 = jnp.float32
BF16 = jnp.bfloat16


def _dot(a, b):
    return jnp.dot(a, b, preferred_element_type=F32)


def _dot_nt(a, b):
    return lax.dot_general(a, b, (((1,), (1,)), ((), ())), preferred_element_type=F32)


def _sigmoid(x):
    return 1.0 / (1.0 + jnp.exp(-x))


def _silu(x):
    return x * _sigmoid(x)


def _log_sigmoid(x):
    return jnp.minimum(x, 0.0) - jnp.log1p(jnp.exp(-jnp.abs(x)))


def _params(semantics):
    return pltpu.CompilerParams(dimension_semantics=semantics, vmem_limit_bytes=VMEM_LIMIT)


def _inproj0_kernel(x_ref, w_ref, wvt_ref, wg_ref, bg_ref, tri_ref,
                    xa_ref, zas_ref, xb_ref, vt_ref, so_ref, zs_ref, gcol_ref, grow_ref):
    x = x_ref[...].astype(BF16)
    W = POOL_WIDTH

    pre = _dot(x, wg_ref[...])
    ig = pre + bg_ref[0:1, :]
    lf = _log_sigmoid(pltpu.roll(pre, LANES - 8, axis=1) + bg_ref[1:2, :])
    lane = lax.broadcasted_iota(jnp.int32, pre.shape, 1)
    hi = lf.astype(BF16)
    lo = pltpu.roll(lf - hi.astype(F32), 8, axis=1).astype(BF16)
    packed = jnp.where(lane < 8, hi, lo)
    L = MLSTM_CHUNK
    gs, tots = [], []
    for c in range(pre.shape[0] // L):
        part = _dot(tri_ref[...], packed[c * L:(c + 1) * L])
        part = part + pltpu.roll(part, LANES - 8, axis=1)
        gs.append(part)
        tots.append(jnp.broadcast_to(part[L - 1:L], part.shape))
    g = jnp.concatenate(gs, axis=0)
    tot = jnp.concatenate(tots, axis=0)
    gsel = jnp.where(lane < MLSTM_HEADS, g, tot - g + lf)
    r = ig - gsel
    cm = jnp.where(lane < 8, r, jnp.where(lane < 16, pltpu.roll(tot, 8, axis=1), pltpu.roll(gsel, 16, axis=1)))
    tr = cm.T
    H = MLSTM_HEADS
    for h in range(H):
        sh = (LANES - h) % LANES
        gcol_ref[:, h * LANES:(h + 1) * LANES] = pltpu.roll(r, sh, axis=1) if sh else r
        picks = [tr[q * 2 * H + d * H + h:q * 2 * H + d * H + h + 1] for q in range(3) for d in range(N_DIRS)]
        grow_ref[h * GATE_ROWS:(h + 1) * GATE_ROWS, :] = jnp.concatenate(picks + [tr[6 * H:6 * H + 2]], axis=0)

    xa_ref[...] = _dot(x, w_ref[:, 0 * W:1 * W])
    zas_ref[...] = _silu(_dot(x, w_ref[:, 1 * W:2 * W])).astype(BF16)
    xb_ref[...] = _dot(x, w_ref[:, 2 * W:3 * W])
    vt_ref[...] = _dot_nt(wvt_ref[...], x).astype(BF16)
    so_ref[...] = _sigmoid(_dot(x, w_ref[:, 3 * W:4 * W])).astype(BF16)
    zs_ref[...] = _silu(_dot(x, w_ref[:, 4 * W:5 * W])).astype(BF16)


def _inproj0(x2d, w_main, w_vt, w_gate, b_gate, tri, tm):
    n = x2d.shape[0]
    W = POOL_WIDTH
    tok = lambda width: pl.BlockSpec((tm, width), lambda i: (i, 0))
    full = lambda a: pl.BlockSpec(a.shape, lambda i: (0,) * a.ndim)
    return pl.pallas_call(
        _inproj0_kernel,
        grid=(n // tm,),
        in_specs=[tok(D_MODEL), full(w_main), full(w_vt), full(w_gate), full(b_gate), full(tri)],
        out_specs=[tok(W), tok(W), tok(W), pl.BlockSpec((W, tm), lambda i: (0, i)), tok(W), tok(W),
                   tok(MLSTM_HEADS * LANES),
                   pl.BlockSpec((MLSTM_HEADS * GATE_ROWS, tm), lambda i: (0, i))],
        out_shape=[
            jax.ShapeDtypeStruct((n, W), F32),
            jax.ShapeDtypeStruct((n, W), BF16),
            jax.ShapeDtypeStruct((n, W), F32),
            jax.ShapeDtypeStruct((W, n), BF16),
            jax.ShapeDtypeStruct((n, W), BF16),
            jax.ShapeDtypeStruct((n, W), BF16),
            jax.ShapeDtypeStruct((n, MLSTM_HEADS * LANES), F32),
            jax.ShapeDtypeStruct((MLSTM_HEADS * GATE_ROWS, n), F32),
        ],
        compiler_params=_params(("parallel",)),
        name="inproj_even",
    )(x2d, w_main, w_vt, w_gate, b_gate, tri)


def _mix0_kernel(xa_ref, xap_ref, xan_ref, xb_ref, xbp_ref, xbn_ref, zas_ref,
                 wpool_ref, pscale_ref, convw_ref, convb_ref, wk_ref, wqt_ref,
                 outa_ref, k_ref, qt_ref, xc_ref, exta_ref, extb_ref, *, seq_len, tm):
    i = pl.program_id(1)
    first = i == 0
    last = i == pl.num_programs(1) - 1
    for ext, main, prev, nxt in ((exta_ref, xa_ref, xap_ref, xan_ref), (extb_ref, xb_ref, xbp_ref, xbn_ref)):
        ext[0:HALO, :] = jnp.where(first, 0.0, prev[...])
        ext[HALO:HALO + tm, :] = main[...]
        ext[HALO + tm:HALO + tm + HALO, :] = jnp.where(last, 0.0, nxt[...])

    t = i * tm + lax.broadcasted_iota(jnp.int32, (tm, 1), 0)
    GD = POOL_GROUP_DIM
    for g, w in enumerate(POOL_WINDOWS):
        left = w // 2
        right = w - 1 - left
        cols = slice(g * GD, (g + 1) * GD)
        acc = exta_ref[HALO - left:HALO - left + tm, cols]
        for off in range(-left + 1, right + 1):
            acc = acc + exta_ref[HALO + off:HALO + off + tm, cols]
        cnt = (jnp.minimum(t + right + 1, seq_len) - jnp.maximum(t - left, 0)).astype(F32)
        pooled = acc / cnt - xa_ref[:, cols]
        mixed = _dot(pooled.astype(BF16), wpool_ref[g])
        outa_ref[:, cols] = (mixed * pscale_ref[:, cols] * zas_ref[:, cols].astype(F32)).astype(BF16)

    conv = (extb_ref[HALO - 1:HALO - 1 + tm, :] * convw_ref[0:1, :]
            + extb_ref[HALO:HALO + tm, :] * convw_ref[1:2, :]
            + extb_ref[HALO + 1:HALO + 1 + tm, :] * convw_ref[2:3, :]
            + convb_ref[...])
    xc = _silu(conv)
    xc_ref[...] = xc.astype(BF16)
    xcb = xc.astype(BF16)
    dh = MLSTM_HEAD_DIM
    for h in range(MLSTM_HEADS):
        cols = slice(h * dh, (h + 1) * dh)
        k_ref[:, cols] = (_dot(xcb[:, cols], wk_ref[h]) * (dh ** -0.5)).astype(BF16)
        qt_ref[cols, :] = _dot_nt(wqt_ref[h], xcb[:, cols]).astype(BF16)


def _mix0(xa, xb, zas, w_pool, pool_scale, conv_w, conv_b, w_k, w_qt, batch, seq_len, tm):
    n = xa.shape[0]
    W = POOL_WIDTH
    nt = seq_len // tm
    hb = tm // HALO
    nhalo = n // HALO
    main = pl.BlockSpec((tm, W), lambda b, i: (b * nt + i, 0))
    prev = pl.BlockSpec((HALO, W), lambda b, i: (jnp.maximum((b * nt + i) * hb - 1, 0), 0))
    nxt = pl.BlockSpec((HALO, W), lambda b, i: (jnp.minimum((b * nt + i + 1) * hb, nhalo - 1), 0))
    full = lambda a: pl.BlockSpec(a.shape, lambda b, i: (0,) * a.ndim)
    return pl.pallas_call(
        functools.partial(_mix0_kernel, seq_len=seq_len, tm=tm),
        grid=(batch, nt),
        in_specs=[main, prev, nxt, main, prev, nxt, main,
                  full(w_pool), full(pool_scale), full(conv_w), full(conv_b), full(w_k), full(w_qt)],
        out_specs=[main, main, pl.BlockSpec((W, tm), lambda b, i: (b, i)), main],
        out_shape=[
            jax.ShapeDtypeStruct((n, W), BF16),
            jax.ShapeDtypeStruct((n, W), BF16),
            jax.ShapeDtypeStruct((batch * W, seq_len), BF16),
            jax.ShapeDtypeStruct((n, W), BF16),
        ],
        scratch_shapes=[pltpu.VMEM((tm + 2 * HALO, W), F32), pltpu.VMEM((tm + 2 * HALO, W), F32)],
        compiler_params=_params(("parallel", "parallel")),
        name="mix_even",
    )(xa, xa, xa, xb, xb, xb, zas, w_pool, pool_scale, conv_w, conv_b, w_k, w_qt)


def _mlstm_chunk(k, qt, vt, r_col, r_row, tot_row, g_row, st, m, mask):
    L = MLSTM_CHUNK
    one_row = (lax.broadcasted_iota(jnp.int32, (STATE_ROWS - LANES, L), 0) == 0).astype(BF16)
    vext = jnp.concatenate([vt, one_row], axis=0)
    both = _dot(jnp.concatenate([k, st.astype(BF16)], axis=0), qt)
    kq = both[:L]
    inter = both[L:]
    dm = jnp.where(mask, r_col + g_row, -jnp.inf)
    inter_m = g_row + m
    m_row = jnp.maximum(inter_m, jnp.max(dm, axis=0, keepdims=True))
    p = kq * jnp.exp(dm - m_row)
    intra = _dot(vext, p.astype(BF16))
    ex = jnp.exp(inter_m - m_row)
    num = intra[:LANES] + ex * inter[:LANES]
    den = intra[LANES:LANES + 1] + ex * inter[LANES:LANES + 1]
    h_t = num / jnp.maximum(jnp.abs(den), jnp.exp(-m_row))
    big_g = tot_row[:, 0:1]
    m_new = big_g + jnp.maximum(m, jnp.max(r_row, axis=-1, keepdims=True))
    decay = jnp.exp(big_g + m - m_new)
    ws = jnp.exp(big_g + r_row - m_new)
    st_new = decay * st + _dot((vext.astype(F32) * ws).astype(BF16), k)
    return h_t.T, st_new, m_new


def _mlstm_kernel(k_ref, qt_ref, vt_ref, gcol_ref, grow_ref, so_ref, xc_ref, zs_ref, ng_ref, skip_ref,
                  out_ref, stf_ref, stb_ref, hf_ref, hb_ref, *, seq_len):
    L = MLSTM_CHUNK
    nc = seq_len // L
    U = MLSTM_UNROLL
    stf_ref[...] = jnp.zeros_like(stf_ref)
    stb_ref[...] = jnp.zeros_like(stb_ref)
    row = lax.broadcasted_iota(jnp.int32, (L, L), 0)
    col = lax.broadcasted_iota(jnp.int32, (L, L), 1)
    causal = row <= col
    anti = row >= col

    def load(o, d):
        lane = d * MLSTM_HEADS
        return (k_ref[pl.ds(o, L), :], qt_ref[:, pl.ds(o, L)], vt_ref[:, pl.ds(o, L)],
                gcol_ref[pl.ds(o, L), lane:lane + 1], grow_ref[d:d + 1, pl.ds(o, L)],
                grow_ref[2 + d:3 + d, pl.ds(o, L)], grow_ref[4 + d:5 + d, pl.ds(o, L)])

    def body(j, carry):
        m_f, m_b = carry
        offs_f = [pl.multiple_of((j * U + u) * L, L) for u in range(U)]
        offs_b = [pl.multiple_of((nc - 1 - j * U - u) * L, L) for u in range(U)]
        ins_f = [load(o, 0) for o in offs_f]
        ins_b = [load(o, 1) for o in offs_b]
        st_f = stf_ref[...]
        st_b = stb_ref[...]
        hs_f, hs_b = [], []
        for u in range(U):
            h, st_f, m_f = _mlstm_chunk(*ins_f[u], st_f, m_f, causal)
            hs_f.append(h)
            h, st_b, m_b = _mlstm_chunk(*ins_b[u], st_b, m_b, anti)
            hs_b.append(h)
        for u in range(U):
            hf_ref[pl.ds(offs_f[u], L), :] = hs_f[u]
            hb_ref[pl.ds(offs_b[u], L), :] = hs_b[u]
        stf_ref[...] = st_f
        stb_ref[...] = st_b
        return m_f, m_b

    m0 = jnp.full((1, 1), M_INIT, F32)
    lax.fori_loop(0, nc // U, body, (m0, m0))

    rows = 512

    def fin(j, _):
        o = pl.multiple_of(j * rows, rows)
        sl = pl.ds(o, rows)
        h = (hf_ref[sl, :] + hb_ref[sl, :]) * so_ref[sl, :].astype(F32)
        mu = jnp.mean(h, axis=-1, keepdims=True)
        d = h - mu
        var = jnp.mean(d * d, axis=-1, keepdims=True)
        hn = d * lax.rsqrt(var + LN_EPS) * ng_ref[...]
        out = (hn + skip_ref[...] * xc_ref[sl, :].astype(F32)) * zs_ref[sl, :].astype(F32)
        out_ref[sl, :] = out.astype(BF16)
        return 0

    lax.fori_loop(0, seq_len // rows, fin, 0)


def _mlstm(k, qt, vt, gcol, grow, so, xc, zs, norm_g, skip, batch, seq_len):
    n = k.shape[0]
    H = MLSTM_HEADS
    dh = MLSTM_HEAD_DIM
    tok = pl.BlockSpec((seq_len, dh), lambda b, h: (b, h))
    vec = pl.BlockSpec((1, dh), lambda b, h: (0, h))
    return pl.pallas_call(
        functools.partial(_mlstm_kernel, seq_len=seq_len),
        grid=(batch, H),
        in_specs=[tok, pl.BlockSpec((dh, seq_len), lambda b, h: (b * H + h, 0)),
                  pl.BlockSpec((dh, seq_len), lambda b, h: (h, b)),
                  tok, pl.BlockSpec((GATE_ROWS, seq_len), lambda b, h: (h, b)),
                  tok, tok, tok, vec, vec],
        out_specs=tok,
        out_shape=jax.ShapeDtypeStruct((n, H * dh), BF16),
        scratch_shapes=[pltpu.VMEM((STATE_ROWS, dh), F32), pltpu.VMEM((STATE_ROWS, dh), F32),
                        pltpu.VMEM((seq_len, dh), F32), pltpu.VMEM((seq_len, dh), F32)],
        compiler_params=_params(("parallel", "parallel")),
        name="mlstm",
    )(k, qt, vt, gcol, grow, so, xc, zs, norm_g, skip)


def _outproj_ln_kernel(a_ref, b_ref, wa_ref, wb_ref, x_ref, g_ref, beta_ref, y_ref):
    mix = _dot(a_ref[...], wa_ref[...]) + _dot(b_ref[...], wb_ref[...])
    z = ALPHA * x_ref[...] + mix
    mu = jnp.mean(z, axis=-1, keepdims=True)
    d = z - mu
    var = jnp.mean(d * d, axis=-1, keepdims=True)
    y_ref[...] = d * lax.rsqrt(var + LN_EPS) * g_ref[...] + beta_ref[...]


def _outproj_ln(a, a_blk, b, b_blk, w_a, w_b, x2d, ln_g, ln_b, tm):
    n = x2d.shape[0]
    half = w_a.shape[0]
    full = lambda arr: pl.BlockSpec(arr.shape, lambda i: (0,) * arr.ndim)
    return pl.pallas_call(
        _outproj_ln_kernel,
        grid=(n // tm,),
        in_specs=[pl.BlockSpec((tm, half), lambda i: (i, a_blk)),
                  pl.BlockSpec((tm, half), lambda i: (i, b_blk)),
                  full(w_a), full(w_b),
                  pl.BlockSpec((tm, D_MODEL), lambda i: (i, 0)), full(ln_g), full(ln_b)],
        out_specs=pl.BlockSpec((tm, D_MODEL), lambda i: (i, 0)),
        out_shape=jax.ShapeDtypeStruct((n, D_MODEL), F32),
        compiler_params=_params(("parallel",)),
        name="outproj_ln",
    )(a, b, w_a, w_b, x2d, ln_g, ln_b)


def _rope(xn, cos, sin_signed):
    return xn * cos + pltpu.roll(xn, LANES // 2, axis=1) * sin_signed


def _inproj1_kernel(x_ref, w_ref, cos_ref, sin_ref, qg_ref, kg_ref, q_ref, k_ref, vext_ref, zs_ref):
    x = x_ref[...].astype(BF16)
    cos = cos_ref[...]
    sin = sin_ref[...]
    tm = x.shape[0]
    dh = HEAD_DIM
    SW = 4 * dh

    def norm_rope(y, gain, scale):
        ms = jnp.mean(y * y, axis=-1, keepdims=True)
        yn = y * lax.rsqrt(ms + RMS_EPS) * gain
        out = _rope(yn, cos, sin)
        return out * scale if scale is not None else out

    def q_epilogue(y, s):
        for h in range(SW // dh):
            cols = slice(s * SW + h * dh, s * SW + (h + 1) * dh)
            q_ref[:, cols] = norm_rope(y[:, h * dh:(h + 1) * dh], qg_ref[...], dh ** -0.5 * LOG2E).astype(BF16)

    def kv_epilogue(y, s):
        one_col = (lax.broadcasted_iota(jnp.int32, (tm, LANES), 1) == 0).astype(BF16)
        for h in range(KV_HEADS):
            k_ref[:, h * dh:(h + 1) * dh] = norm_rope(y[:, h * dh:(h + 1) * dh], kg_ref[...], None).astype(BF16)
            vext_ref[:, 2 * h * dh:(2 * h + 1) * dh] = y[:, KV_WIDTH + h * dh:KV_WIDTH + (h + 1) * dh].astype(BF16)
            vext_ref[:, (2 * h + 1) * dh:(2 * h + 2) * dh] = one_col

    def z_epilogue(y, s):
        zs_ref[:, s * SW:(s + 1) * SW] = _silu(y).astype(BF16)

    nq = ATTN_WIDTH // SW
    z0 = ATTN_WIDTH + 2 * KV_WIDTH
    slabs = [(ATTN_WIDTH, kv_epilogue, 0)]
    for s in range(nq):
        slabs += [(s * SW, q_epilogue, s), (z0 + s * SW, z_epilogue, s)]
    y_next = _dot(x, w_ref[:, slabs[0][0]:slabs[0][0] + SW])
    for i, (_, epilogue, s) in enumerate(slabs):
        y = y_next
        if i + 1 < len(slabs):
            c = slabs[i + 1][0]
            y_next = _dot(x, w_ref[:, c:c + SW])
        epilogue(y, s)


def _inproj1(x2d, w, cos, sin, q_gain, k_gain, seq_len, tm):
    n = x2d.shape[0]
    nt = seq_len // tm
    tok = lambda width: pl.BlockSpec((tm, width), lambda i: (i, 0))
    pos = pl.BlockSpec((tm, LANES), lambda i: (i % nt, 0))
    full = lambda a: pl.BlockSpec(a.shape, lambda i: (0,) * a.ndim)
    return pl.pallas_call(
        _inproj1_kernel,
        grid=(n // tm,),
        in_specs=[tok(D_MODEL), full(w), pos, pos, full(q_gain), full(k_gain)],
        out_specs=[tok(ATTN_WIDTH), tok(KV_WIDTH), tok(2 * KV_WIDTH), tok(ATTN_WIDTH)],
        out_shape=[
            jax.ShapeDtypeStruct((n, ATTN_WIDTH), BF16),
            jax.ShapeDtypeStruct((n, KV_WIDTH), BF16),
            jax.ShapeDtypeStruct((n, 2 * KV_WIDTH), BF16),
            jax.ShapeDtypeStruct((n, ATTN_WIDTH), BF16),
        ],
        compiler_params=_params(("parallel",)),
        name="inproj_odd",
    )(x2d, w, cos, sin, q_gain, k_gain)


def _attn_kernel(q_ref, k_ref, vext_ref, zs_ref, o_ref, s_ref, mrun_ref, mrep_ref, *, seq_len, tk):
    dh = HEAD_DIM
    nk = seq_len // tk
    nl = tk // LANES

    def qk_chunk(h, slot, j):
        s = _dot_nt(q_ref[:, h * dh:(h + 1) * dh], k_ref[j * tk:(j + 1) * tk, :])
        s_ref[slot, :, j * tk:(j + 1) * tk] = s
        m = mrun_ref[...]
        for c in range(nl):
            m = jnp.maximum(m, s[:, c * LANES:(c + 1) * LANES])
        mrun_ref[...] = m

    def finish_max(slot):
        m = jnp.max(mrun_ref[...], axis=-1, keepdims=True)
        mrep_ref[slot] = jnp.broadcast_to(m, mrun_ref.shape)
        mrun_ref[...] = jnp.full_like(mrun_ref, -jnp.inf)

    def pv_chunk(slot, j):
        m = mrep_ref[slot]
        s = s_ref[slot, :, j * tk:(j + 1) * tk]
        p = jnp.concatenate([jnp.exp2(s[:, c * LANES:(c + 1) * LANES] - m) for c in range(nl)], axis=1)
        return _dot(p.astype(BF16), vext_ref[j * tk:(j + 1) * tk, :])

    mrun_ref[...] = jnp.full_like(mrun_ref, -jnp.inf)
    for j in range(nk):
        qk_chunk(0, 0, j)
    finish_max(0)
    for h in range(GQA_GROUP):
        slot = h % 2
        acc = None
        for j in range(nk):
            if h + 1 < GQA_GROUP:
                qk_chunk(h + 1, 1 - slot, j)
            part = pv_chunk(slot, j)
            acc = part if acc is None else acc + part
        if h + 1 < GQA_GROUP:
            finish_max(1 - slot)
        out = acc[:, :dh] / acc[:, dh:dh + 1]
        o_ref[:, h * dh:(h + 1) * dh] = (out * zs_ref[:, h * dh:(h + 1) * dh].astype(F32)).astype(BF16)


def _attention(q, k, vext, zs, batch, seq_len, tq, tk):
    n = q.shape[0]
    dh = HEAD_DIM
    nq = seq_len // tq
    gw = GQA_GROUP * dh
    qspec = pl.BlockSpec((tq, gw), lambda b, g, i: (b * nq + i, g))
    return pl.pallas_call(
        functools.partial(_attn_kernel, seq_len=seq_len, tk=tk),
        grid=(batch, KV_HEADS, nq),
        in_specs=[qspec,
                  pl.BlockSpec((seq_len, dh), lambda b, g, i: (b, g)),
                  pl.BlockSpec((seq_len, 2 * dh), lambda b, g, i: (b, g)),
                  qspec],
        out_specs=qspec,
        out_shape=jax.ShapeDtypeStruct((n, ATTN_WIDTH), BF16),
        scratch_shapes=[pltpu.VMEM((2, tq, seq_len), F32),
                        pltpu.VMEM((tq, LANES), F32),
                        pltpu.VMEM((2, tq, LANES), F32)],
        compiler_params=_params(("parallel", "parallel", "parallel")),
        name="gqa_attn",
    )(q, k, vext, zs)


def _rope_perm():
    f = AXIS_DIM // 2
    return np.concatenate([np.arange(0, f), np.arange(2 * f, 3 * f), np.arange(f, 2 * f), np.arange(3 * f, 4 * f)])


def _rope_tables(seq_len):
    t = np.arange(seq_len)
    n_freq = AXIS_DIM // 2
    inv = jnp.asarray(ROPE_THETA, F32) ** (-jnp.arange(n_freq, dtype=F32) / n_freq)
    t_row = jnp.asarray(t // GRID_W, F32)
    t_col = jnp.asarray(t % GRID_W, F32)
    ang = jnp.concatenate([t_row[:, None] * inv, t_col[:, None] * inv], axis=1)
    cos = jnp.cos(ang)
    sin = jnp.sin(ang)
    return jnp.concatenate([cos, cos], axis=1), jnp.concatenate([-sin, sin], axis=1)


def _permute_heads(w, n_heads):
    idx = np.concatenate([h * HEAD_DIM + _rope_perm() for h in range(n_heads)])
    return w[..., idx]


def _prefix_matrix():
    idx = np.arange(MLSTM_CHUNK)
    return jnp.asarray(idx[None, :] <= idx[:, None], BF16)


def _pad_lanes(a):
    return jnp.pad(a, ((0, 0), (0, LANES - a.shape[-1])))


def _even_layer(x2d, batch, seq_len, p, j, layer):
    W = POOL_WIDTH
    tm = 512
    w_in = p['w_in_even'][j]
    w_main = jnp.concatenate([w_in[:, :3 * W], w_in[:, 4 * W:6 * W]], axis=1).astype(BF16)
    w_vt = w_in[:, 3 * W:4 * W].T.astype(BF16)
    w_gate = _pad_lanes(w_in[:, 6 * W:]).astype(BF16)
    b_gate = jnp.stack([_pad_lanes(p['b_gate_i'][j].reshape(1, -1))[0],
                        _pad_lanes(p['b_gate_f'][j].reshape(1, -1))[0]]).astype(F32)
    xa, zas, xb, vt, so, zs, gcol, grow = _inproj0(x2d, w_main, w_vt, w_gate, b_gate, _prefix_matrix(), tm)
    out_a, k, qt, xc = _mix0(
        xa, xb, zas, p['w_pool'][j].astype(BF16), p['pool_scale'][j].reshape(1, W),
        p['conv_w'][j], p['conv_b'][j].reshape(1, W), p['w_k_m'][j].astype(BF16),
        jnp.swapaxes(p['w_q_m'][j], 1, 2).astype(BF16), batch, seq_len, tm)
    out_b = _mlstm(k, qt, vt, gcol, grow, so, xc, zs, p['mh_norm_g'][j].reshape(1, W),
                   p['skip'][j].reshape(1, W), batch, seq_len)
    w_out = p['w_out_even'][j].astype(BF16)
    return _outproj_ln(out_a, 0, out_b, 0, w_out[:W], w_out[W:], x2d,
                       p['ln_g'][layer].reshape(1, -1), p['ln_b'][layer].reshape(1, -1), 1024)


def _odd_layer(x2d, batch, seq_len, p, j, layer):
    cos, sin = _rope_tables(seq_len)
    w_in = p['w_in_odd'][j]
    w = jnp.concatenate([_permute_heads(w_in[:, :ATTN_WIDTH], ATTN_HEADS),
                         _permute_heads(w_in[:, ATTN_WIDTH:ATTN_WIDTH + KV_WIDTH], KV_HEADS),
                         w_in[:, ATTN_WIDTH + KV_WIDTH:]], axis=1).astype(BF16)
    q, k, vext, zs = _inproj1(x2d, w, cos, sin,
                              _permute_heads(p['q_norm_g'][j].reshape(1, -1), 1),
                              _permute_heads(p['k_norm_g'][j].reshape(1, -1), 1), seq_len, 512)
    tq = 512 if seq_len > 2048 else 1024
    o = _attention(q, k, vext, zs, batch, seq_len, min(tq, seq_len), min(2048, seq_len))
    w_out = p['w_out_odd'][j].astype(BF16)
    half = ATTN_WIDTH // 2
    return _outproj_ln(o, 0, o, 1, w_out[:half], w_out[half:], x2d,
                       p['ln_g'][layer].reshape(1, -1), p['ln_b'][layer].reshape(1, -1), 1024)


def _trunk(x, p):
    batch, seq_len, _ = x.shape
    x2d = x.reshape(batch * seq_len, D_MODEL)
    for layer in range(DEPTH):
        j = layer // 2
        if layer % 2 == 0:
            x2d = _even_layer(x2d, batch, seq_len, p, j, layer)
        else:
            x2d = _odd_layer(x2d, batch, seq_len, p, j, layer)
    return x2d.reshape(batch, seq_len, D_MODEL)


def kernel(x_prompt, x_sample, w_in_even, w_pool, pool_scale, conv_w, conv_b, w_q_m, w_k_m, b_gate_i, b_gate_f, mh_norm_g, skip, w_out_even, w_in_odd, q_norm_g, k_norm_g, w_out_odd, ln_g, ln_b):
    p = dict(w_in_even=w_in_even, w_pool=w_pool, pool_scale=pool_scale, conv_w=conv_w, conv_b=conv_b,
             w_q_m=w_q_m, w_k_m=w_k_m, b_gate_i=b_gate_i, b_gate_f=b_gate_f, mh_norm_g=mh_norm_g,
             skip=skip, w_out_even=w_out_even, w_in_odd=w_in_odd, q_norm_g=q_norm_g, k_norm_g=k_norm_g,
             w_out_odd=w_out_odd, ln_g=ln_g, ln_b=ln_b)
    return _trunk(x_prompt, p), _trunk(x_sample, p)
```
